```python
import math
import jax
import jax.numpy as jnp
from jax import lax
import numpy as np

D_MODEL = 2048
BATCH = 4
SEQ = 8192
DEPTH = 2

N_MEM = 256
N_GROUPS = 4
GROUP_WIDTH = D_MODEL // N_GROUPS
HEADS = 4
HEAD_V = GROUP_WIDTH // HEADS
GLA_DK = HEAD_V // 2
GLA_GATE_RANK = 16
GLA_GATE_NORM = 16.0
HGRN_DK = HEAD_V
DIFF_DK = HEAD_V // 2
MLSTM_DK = HEAD_V // 2
CONV_WIDTH = 4
CHUNK = 64
Q_BLOCK = 128
T5_BUCKETS = 32
T5_MAX_DIST = 128
N_XHEADS = 4
XHEAD_DIM = D_MODEL // N_XHEADS
D_FF = 5632
LN_EPS = 1e-5
LB_EPS = 1e-12
MASK_NEG = -1e30
F32 = jnp.float32

IN_SPLITS = (
    HEADS * GLA_DK, HEADS * GLA_DK, GROUP_WIDTH, GLA_GATE_RANK, GROUP_WIDTH,
    HEADS * HGRN_DK, HEADS * HGRN_DK, GROUP_WIDTH, GROUP_WIDTH,
    2 * HEADS * DIFF_DK, 2 * HEADS * DIFF_DK, GROUP_WIDTH,
    HEADS * MLSTM_DK, HEADS * MLSTM_DK, GROUP_WIDTH, 2 * HEADS, GROUP_WIDTH,
)
N_IN = sum(IN_SPLITS)

kernel_name = 'hybrid_parallel_heads_decoder'


def layer_norm(x, g, b):
    xf = x.astype(F32)
    mu = jnp.mean(xf, -1, keepdims=True)
    var = jnp.mean(jnp.square(xf - mu), -1, keepdims=True)
    return ((xf - mu) * lax.rsqrt(var + LN_EPS) * g.astype(F32) + b.astype(F32)).astype(x.dtype)


def rms_norm(x, g):
    xf = x.astype(F32)
    return xf * lax.rsqrt(jnp.mean(xf * xf, -1, keepdims=True) + LN_EPS) * g.astype(F32)


def swiglu_ffn(x, w_in, w_out):
    gate, up = jnp.split(x @ w_in, 2, axis=-1)
    return (jax.nn.silu(gate) * up) @ w_out


def to_heads(t, h):
    b, s, hd = t.shape
    return t.reshape(b, s, h, hd // h).transpose(0, 2, 1, 3)


def from_heads(t):
    b, h, s, d = t.shape
    return t.transpose(0, 2, 1, 3).reshape(b, s, h * d)


def to_chunks(t):
    b, h, s = t.shape[:3]
    return jnp.moveaxis(t.reshape(b, h, s // CHUNK, CHUNK, *t.shape[3:]), 2, 0)


def from_chunks(t):
    t = jnp.moveaxis(t, 0, 2)
    return t.reshape(t.shape[0], t.shape[1], -1, t.shape[-1])


def chunked_gated_linear_attention(q, k, v, log_g):
    b, h, _, dk = q.shape
    dv = v.shape[-1]
    causal = jnp.tril(jnp.ones((CHUNK, CHUNK), dtype=bool))[:, :, None]

    def step(state, inp):
        qc, kc, vc, gc = inp
        cum = jnp.cumsum(gc.astype(F32), axis=2)
        o_inter = jnp.einsum('bhcd,bhde->bhce', qc * jnp.exp(cum), state)
        diff = cum[:, :, :, None, :] - cum[:, :, None, :, :]
        decay = jnp.where(causal, jnp.exp(jnp.where(causal, diff, 0.0)), 0.0)
        attn = jnp.einsum('bhid,bhjd,bhijd->bhij', qc, kc, decay)
        o_intra = jnp.einsum('bhij,bhje->bhie', attn, vc)
        last = cum[:, :, -1:, :]
        new_state = jnp.exp(last[:, :, 0, :])[..., None] * state + jnp.einsum('bhcd,bhce->bhde', kc * jnp.exp(last - cum), vc)
        return new_state, o_inter + o_intra

    state0 = jnp.zeros((b, h, dk, dv), F32)
    _, o = lax.scan(step, state0, (to_chunks(q), to_chunks(k), to_chunks(v), to_chunks(log_g)))
    return from_chunks(o)


def chunked_mlstm(q, k, v, log_i, log_f):
    b, h, _, dk = q.shape
    dv = v.shape[-1]
    causal = jnp.tril(jnp.ones((CHUNK, CHUNK), dtype=bool))

    def step(carry, inp):
        c_st, n_st, m_st = carry
        qc, kc, vc, ic, fc = inp
        ic = ic.astype(F32)
        cum = jnp.cumsum(fc.astype(F32), axis=-1)
        log_inter = cum + m_st[..., None]
        log_intra = jnp.where(causal, cum[..., :, None] - cum[..., None, :] + ic[..., None, :], MASK_NEG)
        m_t = jnp.maximum(log_inter, jnp.max(log_intra, axis=-1))
        w_inter = jnp.exp(log_inter - m_t)
        w_intra = jnp.where(causal, jnp.exp(log_intra - m_t[..., None]), 0.0)
        scores = jnp.einsum('bhid,bhjd->bhij', qc, kc) * w_intra
        num = w_inter[..., None] * jnp.einsum('bhcd,bhde->bhce', qc, c_st) + jnp.einsum('bhij,bhje->bhie', scores, vc)
        den = w_inter * jnp.einsum('bhcd,bhd->bhc', qc, n_st) + jnp.sum(scores, axis=-1)
        h_t = num / jnp.maximum(jnp.abs(den), jnp.exp(-m_t))[..., None]
        log_last_inter = cum[..., -1] + m_st
        log_last_intra = cum[..., -1:] - cum + ic
        m_new = jnp.maximum(log_last_inter, jnp.max(log_last_intra, axis=-1))
        wk = jnp.exp(log_last_intra - m_new[..., None])
        dec = jnp.exp(log_last_inter - m_new)
        c_new = dec[..., None, None] * c_st + jnp.einsum('bhcd,bhce->bhde', kc * wk[..., None], vc)
        n_new = dec[..., None] * n_st + jnp.einsum('bhcd,bhc->bhd', kc, wk)
        return (c_new, n_new, m_new), h_t

    carry0 = (jnp.zeros((b, h, dk, dv), F32), jnp.zeros((b, h, dk), F32), jnp.zeros((b, h), F32))
    _, o = lax.scan(step, carry0, (to_chunks(q), to_chunks(k), to_chunks(v), to_chunks(log_i), to_chunks(log_f)))
    return from_chunks(o)


def t5_bucket(rel):
    n = jnp.maximum(rel, 0)
    max_exact = T5_BUCKETS // 2
    large = max_exact + (jnp.log(jnp.maximum(n, 1).astype(F32) / max_exact)
                         / math.log(T5_MAX_DIST / max_exact) * (T5_BUCKETS - max_exact)).astype(jnp.int32)
    large = jnp.clip(large, max_exact, T5_BUCKETS - 1)
    return jnp.where(n < max_exact, n, large)


def differential_attention(q, k, v, lam, t5_table, norm_g, lambda_init):
    b, h, s, _ = q.shape
    nb = s // Q_BLOCK
    q1, q2 = jnp.split(q * DIFF_DK ** -0.5, 2, axis=-1)
    k1, k2 = jnp.split(k, 2, axis=-1)
    blocks = lambda t: t.reshape(b, h, nb, Q_BLOCK, DIFF_DK).transpose(2, 0, 1, 3, 4)
    k_pos = jnp.arange(s)

    def block(args):
        i, q1b, q2b = args
        q_pos = i * Q_BLOCK + jnp.arange(Q_BLOCK)
        rel = q_pos[:, None] - k_pos[None, :]
        bias = jnp.transpose(t5_table[t5_bucket(rel)], (2, 0, 1)).astype(F32)
        mask = rel >= 0

        def probs(qq, kk):
            logits = jnp.einsum('bhqd,bhkd->bhqk', qq, kk).astype(F32) + bias
            return jax.nn.softmax(jnp.where(mask, logits, MASK_NEG), axis=-1)

        w = probs(q1b, k1) - lam * probs(q2b, k2)
        return jnp.einsum('bhqk,bhkd->bhqd', w.astype(v.dtype), v)

    out = lax.map(block, (jnp.arange(nb), blocks(q1), blocks(q2)))
    out = out.transpose(1, 2, 0, 3, 4).reshape(b, h, s, -1)
    return rms_norm(out, norm_g) * (1.0 - lambda_init)


def causal_dwconv(x, w):
    return lax.conv_general_dilated(x, w[:, None, :].astype(x.dtype), window_strides=(1,),
                                    padding=[(CONV_WIDTH - 1, 0)], dimension_numbers=('NWC', 'WIO', 'NWC'),
                                    feature_group_count=x.shape[-1])


def hybrid_mixer(h, layer, w_in, w_out, gla_gate_w, gla_gate_b, gla_norm_g, lb, hgrn_norm_g,
                 diff_lambda, diff_norm_g, t5_table, mlstm_conv_w, mlstm_gate_b):
    proj = h @ w_in
    (a_q, a_k, a_v, a_lr, a_r, b_q, b_f, b_i, b_g, c_q, c_k, c_v,
     d_q, d_k, d_v, d_if, d_o) = jnp.split(proj, list(np.cumsum(IN_SPLITS)[:-1]), axis=-1)

    a_logg = jax.nn.log_sigmoid((a_lr @ gla_gate_w + gla_gate_b).astype(F32)) / GLA_GATE_NORM
    o_a = chunked_gated_linear_attention(to_heads(a_q, HEADS) * GLA_DK ** -0.5, to_heads(a_k, HEADS),
                                         to_heads(a_v, HEADS), to_heads(a_logg, HEADS))
    o_a = from_heads(rms_norm(o_a, gla_norm_g)) * jax.nn.silu(a_r.astype(F32))

    f_pre = b_f.astype(F32)
    lb = lb.astype(F32)
    log_f = jnp.logaddexp(jnp.log(jnp.maximum(lb, LB_EPS)), jnp.log1p(-lb) + jax.nn.log_sigmoid(f_pre))
    k_b = (1.0 - lb) * jax.nn.sigmoid(-f_pre)
    o_b = chunked_gated_linear_attention(to_heads(jax.nn.silu(b_q), HEADS) * HGRN_DK ** -0.5, to_heads(k_b, HEADS),
                                         to_heads(b_i, HEADS), to_heads(log_f, HEADS))
    o_b = from_heads(rms_norm(o_b, hgrn_norm_g)) * jax.nn.silu(b_g.astype(F32))

    lambda_init = 0.8 - 0.6 * math.exp(-0.3 * layer)
    lq1, lk1, lq2, lk2 = [diff_lambda[j].astype(F32) for j in range(4)]
    lam = jnp.exp(jnp.sum(lq1 * lk1)) - jnp.exp(jnp.sum(lq2 * lk2)) + lambda_init
    o_c = differential_attention(to_heads(c_q, HEADS), to_heads(c_k, HEADS), to_heads(c_v, HEADS),
                                 lam, t5_table, diff_norm_g, lambda_init)
    o_c = from_heads(o_c)

    qk = jax.nn.silu(causal_dwconv(jnp.concatenate([d_q, d_k], axis=-1), mlstm_conv_w))
    d_qc, d_kc = jnp.split(qk, 2, axis=-1)
    gates = (d_if + mlstm_gate_b).astype(F32)
    log_i = gates[..., :HEADS].transpose(0, 2, 1)
    log_fd = jax.nn.log_sigmoid(gates[..., HEADS:]).transpose(0, 2, 1)
    o_d = chunked_mlstm(to_heads(d_qc, HEADS), to_heads(d_kc, HEADS) * MLSTM_DK ** -0.5, to_heads(d_v, HEADS), log_i, log_fd)
    o_d = from_heads(o_d) * jax.nn.sigmoid(d_o.astype(F32))

    mixed = jnp.concatenate([o_a, o_b, o_c, o_d], axis=-1).astype(h.dtype)
    return mixed @ w_out


def memory_cross_attention(h, mem, w_q, w_kv, w_o):
    q = to_heads(h @ w_q, N_XHEADS)
    k, v = jnp.split(mem @ w_kv, 2, axis=-1)
    k, v = to_heads(k, N_XHEADS), to_heads(v, N_XHEADS)
    logits = jnp.einsum('bhqd,bhkd->bhqk', q, k).astype(F32) * XHEAD_DIM ** -0.5
    p = jax.nn.softmax(logits, axis=-1).astype(v.dtype)
    return from_heads(jnp.einsum('bhqk,bhkd->bhqd', p, v)) @ w_o


def setup_inputs(seed: int = 0) -> dict:
    key = jax.random.key(seed)
    ks = jax.random.split(key, 24)
    nrm = lambda k, shape, scale: scale * jax.random.normal(k, shape, F32)
    beta = (8 * DEPTH) ** -0.25
    mlstm_gate_b = jnp.concatenate(
        [nrm(ks[17], (DEPTH, HEADS), 0.1),
         jnp.linspace(3.0, 6.0, HEADS, dtype=F32)[None, :] + nrm(ks[18], (DEPTH, HEADS), 0.1)], axis=-1)
    return {
        'x': nrm(ks[0], (BATCH, SEQ, D_MODEL), 1.0),
        'mem': nrm(ks[1], (BATCH, N_MEM, D_MODEL), 1.0),
        'ln_g': 1.0 + nrm(ks[2], (DEPTH, 4, D_MODEL), 0.02),
        'ln_b': nrm(ks[3], (DEPTH, 4, D_MODEL), 0.02),
        'ffn_w_in': nrm(ks[4], (DEPTH, 2, D_MODEL, 2 * D_FF), D_MODEL ** -0.5),
        'ffn_w_out': nrm(ks[5], (DEPTH, 2, D_FF, D_MODEL), beta * D_FF ** -0.5),
        'w_in': nrm(ks[6], (DEPTH, D_MODEL, N_IN), D_MODEL ** -0.5),
        'w_out': nrm(ks[7], (DEPTH, D_MODEL, D_MODEL), beta * D_MODEL ** -0.5),
        'gla_gate_w': nrm(ks[8], (DEPTH, GLA_GATE_RANK, HEADS * GLA_DK), GLA_GATE_RANK ** -0.5),
        'gla_gate_b': nrm(ks[9], (DEPTH, HEADS * GLA_DK), 0.1),
        'gla_norm_g': 1.0 + nrm(ks[10], (DEPTH, HEAD_V), 0.02),
        'hgrn_lb': nrm(ks[11], (DEPTH, HEADS * HGRN_DK), 0.5),
        'hgrn_norm_g': 1.0 + nrm(ks[12], (DEPTH, HEAD_V), 0.02),
        'diff_lambda': nrm(ks[13], (DEPTH, 4, DIFF_DK), 0.1),
        'diff_norm_g': 1.0 + nrm(ks[14], (DEPTH, 2 * DIFF_DK), 0.02),
        't5_table': nrm(ks[15], (T5_BUCKETS, HEADS), 0.5),
        'mlstm_conv_w': nrm(ks[16], (DEPTH, CONV_WIDTH, 2 * HEADS * MLSTM_DK), CONV_WIDTH ** -0.5),
        'mlstm_gate_b': mlstm_gate_b,
        'xattn_w_q': nrm(ks[19], (DEPTH, D_MODEL, D_MODEL), D_MODEL ** -0.5),
        'xattn_w_kv': nrm(ks[20], (DEPTH, D_MODEL, 2 * D_MODEL), D_MODEL ** -0.5),
        'xattn_w_o': nrm(ks[21], (DEPTH, D_MODEL, D_MODEL), beta * D_MODEL ** -0.5),
    }


def reference(x, mem, ln_g, ln_b, ffn_w_in, ffn_w_out, w_in, w_out, gla_gate_w, gla_gate_b, gla_norm_g,
              hgrn_lb, hgrn_norm_g, diff_lambda, diff_norm_g, t5_table, mlstm_conv_w, mlstm_gate_b,
              xattn_w_q, xattn_w_kv, xattn_w_o):
    alpha = (2 * DEPTH) ** 0.25
    sm = jax.nn.softmax(hgrn_lb.astype(F32), axis=0)
    lower_bounds = jnp.clip(jnp.cumsum(sm, axis=0) - sm[0], 0.0, 1.0 - 1e-6)
    for l in range(DEPTH):
        x = layer_norm(alpha * x + 0.5 * swiglu_ffn(x, ffn_w_in[l, 0], ffn_w_out[l, 0]), ln_g[l, 0], ln_b[l, 0])
        mix = hybrid_mixer(x, l, w_in[l], w_out[l], gla_gate_w[l], gla_gate_b[l], gla_norm_g[l], lower_bounds[l],
                           hgrn_norm_g[l], diff_lambda[l], diff_norm_g[l], t5_table, mlstm_conv_w[l], mlstm_gate_b[l])
        x = layer_norm(alpha * x + mix, ln_g[l, 1], ln_b[l, 1])
        x = layer_norm(alpha * x + memory_cross_attention(x, mem, xattn_w_q[l], xattn_w_kv[l], xattn_w_o[l]), ln_g[l, 2], ln_b[l, 2])
        x = layer_norm(alpha * x + 0.5 * swiglu_ffn(x, ffn_w_in[l, 1], ffn_w_out[l, 1]), ln_g[l, 3], ln_b[l, 3])
    return x
```

```python
import functools
import math

import numpy as np
import jax
import jax.numpy as jnp
from jax import lax
from jax.experimental import pallas as pl
from jax.experimental.pallas import tpu as pltpu

F32 = jnp.float32
BF16 = jnp.bfloat16

HEADS = 4
N_XHEADS = 4
LANES = 128
GLA_DK = 64
HGRN_DK = 128
DIFF_DK = 64
MLSTM_DK = 64
GLA_GATE_RANK = 16
GLA_GATE_NORM = 16.0
CONV_WIDTH = 4
CHUNK = 64
T5_BUCKETS = 32
T5_MAX_DIST = 128
LN_EPS = 1e-5
LB_EPS = 1e-12
MASK_NEG = -1e30
CONV_PAD = 8

VMEM_LIMIT = 56 * 1024 * 1024


def _cparams(sem):
    return pltpu.CompilerParams(dimension_semantics=sem, vmem_limit_bytes=VMEM_LIMIT)


def _sigmoid(x):
    return 1.0 / (1.0 + jnp.exp(-x))


def _silu(x):
    return x * _sigmoid(x)


def _log_sigmoid(x):
    return jnp.minimum(x, 0.0) - jnp.log1p(jnp.exp(-jnp.abs(x)))


def _layer_norm(y, g, b):
    mu = jnp.mean(y, axis=-1, keepdims=True)
    d = y - mu
    var = jnp.mean(d * d, axis=-1, keepdims=True)
    return d * lax.rsqrt(var + LN_EPS) * g + b


def _rms_norm(y, g):
    return y * lax.rsqrt(jnp.mean(y * y, axis=-1, keepdims=True) + LN_EPS) * g


def _dot(a, b):
    return jnp.dot(a, b, preferred_element_type=F32)


def _dot_nt(a, b):
    return lax.dot_general(a, b, (((1,), (1,)), ((), ())), preferred_element_type=F32)


def _dot_tn(a, b):
    return lax.dot_general(a, b, (((0,), (0,)), ((), ())), preferred_element_type=F32)


def _split3(g):
    hi = g.astype(BF16)
    r1 = g - hi.astype(F32)
    mid = r1.astype(BF16)
    lo = (r1 - mid.astype(F32)).astype(BF16)
    return jnp.concatenate([hi, mid, lo], axis=-1)


def _sum3(e, n):
    return e[:, :n] + e[:, n:2 * n] + e[:, 2 * n:3 * n]


def _ffn_ln_kernel(x_ref, xb_ref, wg_ref, wu_ref, wo_ref, g_ref, b_ref, o_ref, ob_ref, acc_ref, *, alpha):
    j = pl.program_id(1)

    @pl.when(j == 0)
    def _():
        acc_ref[...] = jnp.zeros_like(acc_ref)

    xb = xb_ref[...]
    gate = _dot(xb, wg_ref[...])
    up = _dot(xb, wu_ref[...])
    h = (_silu(gate) * up).astype(BF16)
    acc_ref[...] += _dot(h, wo_ref[...])

    @pl.when(j == pl.num_programs(1) - 1)
    def _():
        y = _layer_norm(alpha * x_ref[...] + 0.5 * acc_ref[...], g_ref[...], b_ref[...])
        o_ref[...] = y
        ob_ref[...] = y.astype(BF16)


def ffn_ln(x, xb, w_in, w_out, g, b, *, alpha, tm=512, tf=512):
    t, d = x.shape
    f = w_out.shape[0]
    tm, tf = min(tm, t), min(tf, f)
    nf = f // tf
    return pl.pallas_call(
        functools.partial(_ffn_ln_kernel, alpha=alpha),
        grid=(t // tm, nf),
        in_specs=[
            pl.BlockSpec((tm, d), lambda i, j: (i, 0)),
            pl.BlockSpec((tm, d), lambda i, j: (i, 0)),
            pl.BlockSpec((d, tf), lambda i, j: (0, j)),
            pl.BlockSpec((d, tf), lambda i, j: (0, j + nf)),
            pl.BlockSpec((tf, d), lambda i, j: (j, 0)),
            pl.BlockSpec((1, d), lambda i, j: (0, 0)),
            pl.BlockSpec((1, d), lambda i, j: (0, 0)),
        ],
        out_specs=[pl.BlockSpec((tm, d), lambda i, j: (i, 0)), pl.BlockSpec((tm, d), lambda i, j: (i, 0))],
        out_shape=[jax.ShapeDtypeStruct((t, d), F32), jax.ShapeDtypeStruct((t, d), BF16)],
        scratch_shapes=[pltpu.VMEM((tm, d), F32)],
        compiler_params=_cparams(("parallel", "arbitrary")),
        name="ffn_ln",
    )(x, xb, w_in, w_in, w_out, g.reshape(1, d), b.reshape(1, d))


def _matmul_kernel(x_ref, w_ref, o_ref):
    o_ref[...] = _dot(x_ref[...], w_ref[...]).astype(o_ref.dtype)


def matmul(xb, w, *, out_dtype=F32, tm=512, tn=None):
    t, k = xb.shape
    n = w.shape[1]
    tm = min(tm, t)
    tn = n if tn is None else min(tn, n)
    return pl.pallas_call(
        _matmul_kernel,
        grid=(n // tn, t // tm),
        in_specs=[pl.BlockSpec((tm, k), lambda j, i: (i, 0)), pl.BlockSpec((k, tn), lambda j, i: (0, j))],
        out_specs=pl.BlockSpec((tm, tn), lambda j, i: (i, j)),
        out_shape=jax.ShapeDtypeStruct((t, n), out_dtype),
        compiler_params=_cparams(("parallel", "parallel")),
        name="matmul",
    )(xb, w)


def _proj_ln_kernel(*refs, alpha, n_parts):
    x_ref = refs[0]
    parts = refs[1:1 + n_parts]
    w_ref, g_ref, b_ref, o_ref, ob_ref = refs[1 + n_parts:]
    kp = parts[0].shape[1]
    acc = alpha * x_ref[...]
    for p in range(n_parts):
        acc = acc + _dot(parts[p][...], w_ref[p * kp:(p + 1) * kp, :])
    y = _layer_norm(acc, g_ref[...], b_ref[...])
    o_ref[...] = y
    ob_ref[...] = y.astype(BF16)


def proj_ln(x, parts, w, g, b, *, alpha, tm=512):
    t, d = x.shape
    tm = min(tm, t)
    n_parts = len(parts)
    kp = parts[0].shape[1]
    return pl.pallas_call(
        functools.partial(_proj_ln_kernel, alpha=alpha, n_parts=n_parts),
        grid=(t // tm,),
        in_specs=[pl.BlockSpec((tm, d), lambda i: (i, 0))]
        + [pl.BlockSpec((tm, kp), lambda i: (i, 0)) for _ in parts]
        + [pl.BlockSpec((n_parts * kp, d), lambda i: (0, 0)),
           pl.BlockSpec((1, d), lambda i: (0, 0)), pl.BlockSpec((1, d), lambda i: (0, 0))],
        out_specs=[pl.BlockSpec((tm, d), lambda i: (i, 0)), pl.BlockSpec((tm, d), lambda i: (i, 0))],
        out_shape=[jax.ShapeDtypeStruct((t, d), F32), jax.ShapeDtypeStruct((t, d), BF16)],
        compiler_params=_cparams(("parallel",)),
        name="proj_ln",
    )(x, *parts, w, g.reshape(1, d), b.reshape(1, d))


_LEVELS = tuple(CHUNK >> (s + 1) for s in range(int(math.log2(CHUNK))))
_DIAG_LEVEL = len(_LEVELS)


def _gla_tables():
    c = CHUNK
    i = np.arange(c)[:, None]
    t = np.arange(c)[None, :]
    blocks = [t <= i, t > i]
    for h in _LEVELS:
        blocks.append((t >= (i // h) * h) & (t <= i))
        blocks.append((t > i) & (t <= (i // h + 1) * h - 1))
    sums = np.concatenate(blocks, axis=0).astype(np.float32)
    msb = np.floor(np.log2(np.maximum(i ^ t, 1))).astype(np.int32)
    lvl = np.where(t < i, len(_LEVELS) - 1 - msb, np.where(t == i, _DIAG_LEVEL, -1)).astype(np.int32)
    return sums, lvl


def _gla_chunk(q, k, v, g, sums_ref, lvl, st_ref):
    c, dk = q.shape
    e = _sum3(_dot(sums_ref[...], _split3(g)), dk)
    st = st_ref[...]
    o = _dot_nt((q * jnp.exp(e[0:c])).astype(BF16), st.astype(BF16))
    attn = jnp.where(lvl == _DIAG_LEVEL, jnp.sum(q * k, axis=-1, keepdims=True), 0.0)
    for l in range(len(_LEVELS)):
        ql = (q * jnp.exp(e[(2 + 2 * l) * c:(3 + 2 * l) * c])).astype(BF16)
        kl = (k * jnp.exp(e[(3 + 2 * l) * c:(4 + 2 * l) * c])).astype(BF16)
        attn = jnp.where(lvl == l, _dot_nt(ql, kl), attn)
    vb = v.astype(BF16)
    o = o + _dot(attn.astype(BF16), vb)
    ks = (k * jnp.exp(e[c:2 * c])).astype(BF16)
    st_ref[...] = st * jnp.exp(e[c - 1:c]) + _dot_tn(vb, ks)
    return o


def _gla_kernel(q_ref, k_ref, v_ref, lr_ref, r_ref, gw_ref, gb_ref, ng_ref, sums_ref, lvl_ref, o_ref, st_ref):
    @pl.when(pl.program_id(2) == 0)
    def _():
        st_ref[...] = jnp.zeros_like(st_ref)

    lvl = lvl_ref[...]
    gw = gw_ref[...]
    gb = gb_ref[...]
    ng = ng_ref[...]

    def body(ci, carry):
        sl = pl.ds(pl.multiple_of(ci * CHUNK, CHUNK), CHUNK)
        g = _log_sigmoid(_dot(lr_ref[sl, :].astype(BF16), gw) + gb) / GLA_GATE_NORM
        o = _gla_chunk(q_ref[sl, :] * GLA_DK ** -0.5, k_ref[sl, :], v_ref[sl, :], g, sums_ref, lvl, st_ref)
        o_ref[sl, :] = (_rms_norm(o, ng) * _silu(r_ref[sl, :])).astype(o_ref.dtype)
        return carry

    lax.fori_loop(0, q_ref.shape[0] // CHUNK, body, 0)


def _hgrn_kernel(q_ref, f_ref, v_ref, r_ref, lb_ref, ng_ref, sums_ref, lvl_ref, o_ref, st_ref):
    @pl.when(pl.program_id(2) == 0)
    def _():
        st_ref[...] = jnp.zeros_like(st_ref)

    lvl = lvl_ref[...]
    ng = ng_ref[...]
    lb = lb_ref[...]
    log_lb = jnp.log(jnp.maximum(lb, LB_EPS))
    log_1mlb = jnp.log1p(-lb)

    def body(ci, carry):
        sl = pl.ds(pl.multiple_of(ci * CHUNK, CHUNK), CHUNK)
        f_pre = f_ref[sl, :]
        a = log_lb
        b = log_1mlb + _log_sigmoid(f_pre)
        g = jnp.maximum(a, b) + jnp.log1p(jnp.exp(-jnp.abs(a - b)))
        k = (1.0 - lb) * _sigmoid(-f_pre)
        q = _silu(q_ref[sl, :]) * HGRN_DK ** -0.5
        o = _gla_chunk(q, k, v_ref[sl, :], g, sums_ref, lvl, st_ref)
        o_ref[sl, :] = (_rms_norm(o, ng) * _silu(r_ref[sl, :])).astype(o_ref.dtype)
        return carry

    lax.fori_loop(0, q_ref.shape[0] // CHUNK, body, 0)


def _head_block(tt, nt, col0):
    return pl.BlockSpec((tt, LANES), lambda b, h, t: (b * nt + t, col0 + h))


def _const_block(shape):
    return pl.BlockSpec(shape, lambda b, h, t: tuple(0 for _ in shape))


def gla_mixer(proj, gate_w, gate_b, norm_g, *, batch, tt=512):
    t = proj.shape[0]
    s = t // batch
    tt = min(tt, s)
    nt = s // tt
    sums, lvl = _gla_tables()
    hb = functools.partial(_head_block, tt, nt)
    return pl.pallas_call(
        _gla_kernel,
        grid=(batch, HEADS, nt),
        in_specs=[hb(0), hb(HEADS), hb(2 * HEADS),
                  pl.BlockSpec((tt, LANES), lambda b, h, t: (b * nt + t, 3 * HEADS)),
                  hb(3 * HEADS + 1),
                  pl.BlockSpec((LANES, LANES), lambda b, h, t: (0, h)),
                  pl.BlockSpec((1, LANES), lambda b, h, t: (0, h)),
                  _const_block((1, LANES)), _const_block(sums.shape), _const_block(lvl.shape)],
        out_specs=hb(0),
        out_shape=jax.ShapeDtypeStruct((t, HEADS * LANES), BF16),
        scratch_shapes=[pltpu.VMEM((LANES, LANES), F32)],
        compiler_params=_cparams(("parallel", "parallel", "arbitrary")),
        name="gla",
    )(proj, proj, proj, proj, proj, gate_w, gate_b, norm_g.reshape(1, LANES),
      jnp.asarray(sums, BF16), jnp.asarray(lvl))


def hgrn_mixer(proj, lower_bound, norm_g, *, batch, tt=512):
    t = proj.shape[0]
    s = t // batch
    tt = min(tt, s)
    nt = s // tt
    sums, lvl = _gla_tables()
    hb = functools.partial(_head_block, tt, nt)
    return pl.pallas_call(
        _hgrn_kernel,
        grid=(batch, HEADS, nt),
        in_specs=[hb(0), hb(HEADS), hb(2 * HEADS), hb(3 * HEADS),
                  pl.BlockSpec((1, LANES), lambda b, h, t: (0, h)),
                  _const_block((1, LANES)), _const_block(sums.shape), _const_block(lvl.shape)],
        out_specs=hb(0),
        out_shape=jax.ShapeDtypeStruct((t, HEADS * LANES), BF16),
        scratch_shapes=[pltpu.VMEM((LANES, LANES), F32)],
        compiler_params=_cparams(("parallel", "parallel", "arbitrary")),
        name="hgrn",
    )(proj, proj, proj, proj, lower_bound.reshape(1, HEADS * LANES), norm_g.reshape(1, LANES),
      jnp.asarray(sums, BF16), jnp.asarray(lvl))


def _mlstm_kernel(q_ref, k_ref, v_ref, if_ref, og_ref, cw_q_ref, cw_k_ref, gb_ref, o_ref,
                  st_ref, m_ref, qx_ref, kx_ref, qc_ref, kc_ref):
    tt = q_ref.shape[0]
    h = pl.program_id(1)

    @pl.when(pl.program_id(2) == 0)
    def _():
        st_ref[...] = jnp.zeros_like(st_ref)
        m_ref[...] = jnp.zeros_like(m_ref)
        qx_ref[0:CONV_PAD, :] = jnp.zeros((CONV_PAD, LANES), F32)
        kx_ref[0:CONV_PAD, :] = jnp.zeros((CONV_PAD, LANES), F32)

    for src, ext, cw, dst in ((q_ref, qx_ref, cw_q_ref, qc_ref), (k_ref, kx_ref, cw_k_ref, kc_ref)):
        ext[CONV_PAD:CONV_PAD + tt, :] = src[...]
        acc = jnp.zeros((tt, LANES), F32)
        for w in range(CONV_WIDTH):
            acc = acc + ext[CONV_PAD - (CONV_WIDTH - 1) + w:CONV_PAD - (CONV_WIDTH - 1) + w + tt, :] * cw[w:w + 1, :]
        dst[...] = _silu(acc)
        ext[0:CONV_PAD, :] = ext[tt:tt + CONV_PAD, :]

    c = CHUNK
    row = lax.broadcasted_iota(jnp.int32, (c, c), 0)
    col = lax.broadcasted_iota(jnp.int32, (c, c), 1)
    causal = col <= row
    eye = col == row
    tril = jnp.where(causal, 1.0, 0.0).astype(BF16)
    lane = lax.broadcasted_iota(jnp.int32, (c, LANES), 1)
    ones_col = jnp.where(lax.broadcasted_iota(jnp.int32, (c, LANES), 1) == 0, 1.0, 0.0)
    gbias = gb_ref[...]

    def body(ci, carry):
        sl = pl.ds(pl.multiple_of(ci * c, c), c)
        gates = if_ref[sl, :] + gbias
        i_col = jnp.sum(jnp.where(lane == h, gates, 0.0), axis=-1, keepdims=True)
        f_col = _log_sigmoid(jnp.sum(jnp.where(lane == h + HEADS, gates, 0.0), axis=-1, keepdims=True))
        cum = _sum3(_dot(tril, _split3(jnp.broadcast_to(f_col, (c, LANES)))), LANES)[:, 0:1]
        a_col = i_col - cum
        a_row = jnp.sum(jnp.where(eye, a_col, 0.0), axis=0, keepdims=True)
        m_st = m_ref[...]
        log_intra = jnp.where(causal, cum + a_row, MASK_NEG)
        log_inter = cum + m_st
        m_t = jnp.maximum(log_inter, jnp.max(log_intra, axis=-1, keepdims=True))
        w_inter = jnp.exp(log_inter - m_t)
        w_intra = jnp.where(causal, jnp.exp(log_intra - m_t), 0.0)
        qb = qc_ref[sl, :].astype(BF16)
        kf = kc_ref[sl, :] * MLSTM_DK ** -0.5
        v_ext = jnp.concatenate([v_ref[sl, :], ones_col], axis=-1).astype(BF16)
        scores = _dot_nt(qb, kf.astype(BF16)) * w_intra
        st = st_ref[...]
        nd = w_inter * _dot(qb, st.astype(BF16)) + _dot(scores.astype(BF16), v_ext)
        den = jnp.maximum(jnp.abs(nd[:, LANES:LANES + 1]), jnp.exp(-m_t))
        o_ref[sl, :] = (nd[:, :LANES] / den * _sigmoid(og_ref[sl, :])).astype(o_ref.dtype)
        cum_last = cum[c - 1:c, :]
        lli = cum_last + a_col
        log_last_inter = cum_last + m_st
        m_new = jnp.maximum(log_last_inter, jnp.max(lli, axis=0, keepdims=True))
        wk = jnp.exp(lli - m_new)
        st_ref[...] = jnp.exp(log_last_inter - m_new) * st + _dot_tn((kf * wk).astype(BF16), v_ext)
        m_ref[...] = m_new
        return carry

    lax.fori_loop(0, tt // c, body, 0)


def mlstm_mixer(proj, conv_w_q, conv_w_k, gate_b, *, batch, tt=512):
    t = proj.shape[0]
    s = t // batch
    tt = min(tt, s)
    nt = s // tt
    hb = functools.partial(_head_block, tt, nt)
    return pl.pallas_call(
        _mlstm_kernel,
        grid=(batch, HEADS, nt),
        in_specs=[hb(0), hb(HEADS), hb(2 * HEADS),
                  pl.BlockSpec((tt, LANES), lambda b, h, t: (b * nt + t, 3 * HEADS)),
                  hb(3 * HEADS + 1),
                  pl.BlockSpec((CONV_WIDTH, LANES), lambda b, h, t: (0, h)),
                  pl.BlockSpec((CONV_WIDTH, LANES), lambda b, h, t: (0, h)),
                  _const_block((1, LANES))],
        out_specs=hb(0),
        out_shape=jax.ShapeDtypeStruct((t, HEADS * LANES), BF16),
        scratch_shapes=[pltpu.VMEM((LANES, 2 * LANES), F32), pltpu.VMEM((1, 1), F32),
                        pltpu.VMEM((CONV_PAD + tt, LANES), F32), pltpu.VMEM((CONV_PAD + tt, LANES), F32),
                        pltpu.VMEM((tt, LANES), F32), pltpu.VMEM((tt, LANES), F32)],
        compiler_params=_cparams(("parallel", "parallel", "arbitrary")),
        name="mlstm",
    )(proj, proj, proj, proj, proj, conv_w_q, conv_w_k, gate_b)


def _diff_attn_kernel(sc_ref, q_ref, k_ref, v_ref, bias_ref, ng_ref, o_ref,
                      m1_ref, l1_ref, a1_ref, m2_ref, l2_ref, a2_ref, *, tk):
    tq = q_ref.shape[0]
    qi = pl.program_id(2)
    q = q_ref[...].astype(F32)
    lane = lax.broadcasted_iota(jnp.int32, q.shape, 1)
    q1 = jnp.where(lane < DIFF_DK, q, 0.0).astype(BF16)
    q2 = jnp.where(lane >= DIFF_DK, q, 0.0).astype(BF16)
    for m_ref, l_ref, a_ref in ((m1_ref, l1_ref, a1_ref), (m2_ref, l2_ref, a2_ref)):
        m_ref[...] = jnp.full(m_ref.shape, MASK_NEG, F32)
        l_ref[...] = jnp.zeros_like(l_ref)
        a_ref[...] = jnp.zeros_like(a_ref)

    def body(ki, carry):
        ks = pl.ds(pl.multiple_of(ki * tk, tk), tk)
        kk = k_ref[ks, :]
        vv = v_ref[ks, :]
        bias = bias_ref[jnp.minimum(qi - ki, 2)]
        for qq, m_ref, l_ref, a_ref in ((q1, m1_ref, l1_ref, a1_ref), (q2, m2_ref, l2_ref, a2_ref)):
            s = _dot_nt(qq, kk) * DIFF_DK ** -0.5 + bias
            m_old = m_ref[...]
            m_new = jnp.maximum(m_old, jnp.max(s, axis=-1, keepdims=True))
            p = jnp.exp(s - m_new)
            corr = jnp.exp(m_old - m_new)
            l_ref[...] = l_ref[...] * corr + jnp.sum(p, axis=-1, keepdims=True)
            a_ref[...] = a_ref[...] * corr + _dot(p.astype(BF16), vv)
            m_ref[...] = m_new
        return carry

    lax.fori_loop(0, qi + 1, body, 0)
    lam = sc_ref[0]
    out = a1_ref[...] / l1_ref[...] - lam * (a2_ref[...] / l2_ref[...])
    o_ref[...] = (_rms_norm(out, ng_ref[...]) * sc_ref[1]).astype(o_ref.dtype)


def diff_attn_mixer(proj, bias_tiles, scalars, norm_g, *, batch, tq=512):
    t = proj.shape[0]
    s = t // batch
    tq = min(tq, s)
    nt = s // tq
    return pl.pallas_call(
        functools.partial(_diff_attn_kernel, tk=tq),
        grid=(batch, HEADS, nt),
        in_specs=[pl.BlockSpec(memory_space=pltpu.SMEM),
                  pl.BlockSpec((tq, LANES), lambda b, h, t: (b * nt + t, h)),
                  pl.BlockSpec((s, LANES), lambda b, h, t: (b, HEADS + h)),
                  pl.BlockSpec((s, LANES), lambda b, h, t: (b, 2 * HEADS + h)),
                  pl.BlockSpec((None, 3, tq, tq), lambda b, h, t: (h, 0, 0, 0)),
                  _const_block((1, LANES))],
        out_specs=pl.BlockSpec((tq, LANES), lambda b, h, t: (b * nt + t, h)),
        out_shape=jax.ShapeDtypeStruct((t, HEADS * LANES), BF16),
        scratch_shapes=[pltpu.VMEM((tq, 1), F32), pltpu.VMEM((tq, 1), F32), pltpu.VMEM((tq, LANES), F32),
                        pltpu.VMEM((tq, 1), F32), pltpu.VMEM((tq, 1), F32), pltpu.VMEM((tq, LANES), F32)],
        compiler_params=_cparams(("parallel", "parallel", "arbitrary")),
        name="diff_attn",
    )(scalars, proj, proj, proj, bias_tiles, norm_g.reshape(1, LANES))


def _t5_bucket(rel):
    n = jnp.maximum(rel, 0)
    max_exact = T5_BUCKETS // 2
    large = max_exact + (jnp.log(jnp.maximum(n, 1).astype(F32) / max_exact)
                         / math.log(T5_MAX_DIST / max_exact) * (T5_BUCKETS - max_exact)).astype(jnp.int32)
    large = jnp.clip(large, max_exact, T5_BUCKETS - 1)
    return jnp.where(n < max_exact, n, large)


def t5_bias_tiles(t5_table, tq):
    assert tq >= T5_MAX_DIST
    i = jnp.arange(tq)[:, None]
    j = jnp.arange(tq)[None, :]
    tiles = []
    for d in range(3):
        rel = d * tq + i - j
        b = jnp.transpose(t5_table[_t5_bucket(rel)], (2, 0, 1)).astype(F32)
        tiles.append(jnp.where(rel >= 0, b, MASK_NEG))
    return jnp.stack(tiles, axis=1)


def _xattn_kernel(xb_ref, wq_ref, k_ref, v_ref, o_ref):
    d = xb_ref.shape[1]
    hd = d // N_XHEADS
    q = _dot(xb_ref[...], wq_ref[...]).astype(BF16)
    for h in range(N_XHEADS):
        hs = slice(h * hd, (h + 1) * hd)
        s = _dot_nt(q[:, hs], k_ref[:, hs]) * hd ** -0.5
        s = s - jnp.max(s, axis=-1, keepdims=True)
        p = jnp.exp(s)
        p = p / jnp.sum(p, axis=-1, keepdims=True)
        o_ref[:, hs] = _dot(p.astype(BF16), v_ref[:, hs]).astype(o_ref.dtype)


def xattn(xb, wq, kv, *, batch, tm=512):
    t, d = xb.shape
    s = t // batch
    tm = min(tm, s)
    nt = s // tm
    n_mem = kv.shape[0] // batch
    return pl.pallas_call(
        _xattn_kernel,
        grid=(t // tm,),
        in_specs=[pl.BlockSpec((tm, d), lambda i: (i, 0)),
                  pl.BlockSpec((d, d), lambda i: (0, 0)),
                  pl.BlockSpec((n_mem, d), lambda i: (i // nt, 0)),
                  pl.BlockSpec((n_mem, d), lambda i: (i // nt, 1))],
        out_specs=pl.BlockSpec((tm, d), lambda i: (i, 0)),
        out_shape=jax.ShapeDtypeStruct((t, d), BF16),
        compiler_params=_cparams(("parallel",)),
        name="xattn",
    )(xb, wq, kv, kv)


def _pad_heads(w, dk):
    lead = w.shape[:-1]
    w = w.reshape(*lead, HEADS, dk)
    w = jnp.pad(w, [(0, 0)] * len(lead) + [(0, 0), (0, LANES - dk)])
    return w.reshape(*lead, HEADS * LANES)


def _pad_cols(w, n):
    return jnp.pad(w, [(0, 0)] * (w.ndim - 1) + [(0, n - w.shape[-1])])


def _split_in_proj(w_in):
    gw = HEADS * LANES
    sizes = (HEADS * GLA_DK, HEADS * GLA_DK, gw, GLA_GATE_RANK, gw,
             HEADS * HGRN_DK, HEADS * HGRN_DK, gw, gw,
             2 * HEADS * DIFF_DK, 2 * HEADS * DIFF_DK, gw,
             HEADS * MLSTM_DK, HEADS * MLSTM_DK, gw, 2 * HEADS, gw)
    assert sum(sizes) == w_in.shape[-1]
    (a_q, a_k, a_v, a_lr, a_r, b_q, b_f, b_i, b_g, c_q, c_k, c_v,
     d_q, d_k, d_v, d_if, d_o) = jnp.split(w_in, list(np.cumsum(sizes)[:-1]), axis=-1)
    w_gla = jnp.concatenate([_pad_heads(a_q, GLA_DK), _pad_heads(a_k, GLA_DK), a_v, _pad_cols(a_lr, LANES), a_r], -1)
    w_hgrn = jnp.concatenate([b_q, b_f, b_i, b_g], -1)
    w_diff = jnp.concatenate([c_q, c_k, c_v], -1)
    w_mlstm = jnp.concatenate([_pad_heads(d_q, MLSTM_DK), _pad_heads(d_k, MLSTM_DK), d_v, _pad_cols(d_if, LANES), d_o], -1)
    return [w.astype(BF16) for w in (w_gla, w_hgrn, w_diff, w_mlstm)]


def kernel(x, mem, ln_g, ln_b, ffn_w_in, ffn_w_out, w_in, w_out, gla_gate_w, gla_gate_b, gla_norm_g, hgrn_lb, hgrn_norm_g, diff_lambda, diff_norm_g, t5_table, mlstm_conv_w, mlstm_gate_b, xattn_w_q, xattn_w_kv, xattn_w_o):
    batch, seq, d = x.shape
    depth = ln_g.shape[0]
    alpha = (2 * depth) ** 0.25
    t = batch * seq

    sm = jax.nn.softmax(hgrn_lb.astype(F32), axis=0)
    lower_bounds = jnp.clip(jnp.cumsum(sm, axis=0) - sm[0], 0.0, 1.0 - 1e-6)
    tq = min(512, seq)
    bias_tiles = t5_bias_tiles(t5_table, tq)

    x = x.reshape(t, d)
    xb = x.astype(BF16)
    memb = mem.reshape(-1, d).astype(BF16)
    for l in range(depth):
        x, xb = ffn_ln(x, xb, ffn_w_in[l, 0].astype(BF16), ffn_w_out[l, 0].astype(BF16), ln_g[l, 0], ln_b[l, 0], alpha=alpha)

        w_gla, w_hgrn, w_diff, w_mlstm = _split_in_proj(w_in[l])
        o_a = gla_mixer(matmul(xb, w_gla),
                        _pad_heads(_pad_cols(gla_gate_w[l].T, LANES).T, GLA_DK).astype(BF16),
                        _pad_heads(gla_gate_b[l].reshape(1, -1), GLA_DK), gla_norm_g[l], batch=batch)
        o_b = hgrn_mixer(matmul(xb, w_hgrn), lower_bounds[l], hgrn_norm_g[l], batch=batch)
        lambda_init = 0.8 - 0.6 * math.exp(-0.3 * l)
        lq1, lk1, lq2, lk2 = [diff_lambda[l, j].astype(F32) for j in range(4)]
        lam = jnp.exp(jnp.sum(lq1 * lk1)) - jnp.exp(jnp.sum(lq2 * lk2)) + lambda_init
        o_c = diff_attn_mixer(matmul(xb, w_diff, out_dtype=BF16), bias_tiles,
                              jnp.stack([lam, jnp.asarray(1.0 - lambda_init, F32)]).astype(F32), diff_norm_g[l], batch=batch, tq=tq)
        cw = mlstm_conv_w[l]
        o_d = mlstm_mixer(matmul(xb, w_mlstm), _pad_heads(cw[:, :HEADS * MLSTM_DK], MLSTM_DK),
                          _pad_heads(cw[:, HEADS * MLSTM_DK:], MLSTM_DK),
                          _pad_cols(mlstm_gate_b[l].reshape(1, -1), LANES), batch=batch)
        x, xb = proj_ln(x, [o_a, o_b, o_c, o_d], w_out[l].astype(BF16), ln_g[l, 1], ln_b[l, 1], alpha=alpha)

        kv = matmul(memb, xattn_w_kv[l].astype(BF16), out_dtype=BF16, tn=1024)
        att = xattn(xb, xattn_w_q[l].astype(BF16), kv, batch=batch)
        x, xb = proj_ln(x, [att], xattn_w_o[l].astype(BF16), ln_g[l, 2], ln_b[l, 2], alpha=alpha)

        x, xb = ffn_ln(x, xb, ffn_w_in[l, 1].astype(BF16), ffn_w_out[l, 1].astype(BF16), ln_g[l, 3], ln_b[l, 3], alpha=alpha)
    return x.reshape(batch, seq, d)
```

```python
import functools
import math

import numpy as np
import jax
import jax.numpy as jnp
from jax import lax
from jax.experimental import pallas as pl
from jax.experimental.pallas import tpu as pltpu

F32 = jnp.float32
BF16 = jnp.bfloat16

HEADS = 4
N_XHEADS = 4
LANES = 128
BF16_SUBLANES = 16
GLA_DK = 64
HGRN_DK = 128
DIFF_DK = 64
MLSTM_DK = 64
GLA_GATE_RANK = 16
GLA_GATE_NORM = 16.0
CONV_WIDTH = 4
CHUNK = 64
T5_BUCKETS = 32
T5_MAX_DIST = 128
LN_EPS = 1e-5
LB_EPS = 1e-12
MASK_NEG = -1e30
CONV_PAD = 8

VMEM_LIMIT = 56 * 1024 * 1024


def _cparams(sem):
    return pltpu.CompilerParams(dimension_semantics=sem, vmem_limit_bytes=VMEM_LIMIT)


def _sigmoid(x):
    return 1.0 / (1.0 + jnp.exp(-x))


def _silu(x):
    return x * _sigmoid(x)


def _log_sigmoid(x):
    return jnp.minimum(x, 0.0) - jnp.log1p(jnp.exp(-jnp.abs(x)))


def _layer_norm(y, g, b):
    mu = jnp.mean(y, axis=-1, keepdims=True)
    d = y - mu
    var = jnp.mean(d * d, axis=-1, keepdims=True)
    return d * lax.rsqrt(var + LN_EPS) * g + b


def _rms_norm(y, g):
    return y * lax.rsqrt(jnp.mean(y * y, axis=-1, keepdims=True) + LN_EPS) * g


def _dot(a, b):
    return jnp.dot(a, b, preferred_element_type=F32)


def _dot_nt(a, b):
    return lax.dot_general(a, b, (((1,), (1,)), ((), ())), preferred_element_type=F32)


def _dot_tn(a, b):
    return lax.dot_general(a, b, (((0,), (0,)), ((), ())), preferred_element_type=F32)


def _split3(g):
    hi = g.astype(BF16)
    r1 = g - hi.astype(F32)
    mid = r1.astype(BF16)
    lo = (r1 - mid.astype(F32)).astype(BF16)
    return jnp.concatenate([hi, mid, lo], axis=-1)


def _sum3(e, n):
    return e[:, :n] + e[:, n:2 * n] + e[:, 2 * n:3 * n]


def _ffn_ln_kernel(x_ref, xb_ref, wg_ref, wu_ref, wo_ref, g_ref, b_ref, o_ref, ob_ref, acc_ref, *, alpha):
    j = pl.program_id(1)

    @pl.when(j == 0)
    def _():
        acc_ref[...] = jnp.zeros_like(acc_ref)

    xb = xb_ref[...]
    gate = _dot(xb, wg_ref[...])
    up = _dot(xb, wu_ref[...])
    h = (_silu(gate) * up).astype(BF16)
    acc_ref[...] += _dot(h, wo_ref[...])

    @pl.when(j == pl.num_programs(1) - 1)
    def _():
        y = _layer_norm(alpha * x_ref[...] + 0.5 * acc_ref[...], g_ref[...], b_ref[...])
        o_ref[...] = y
        ob_ref[...] = y.astype(BF16)


def ffn_ln(x, xb, w_in, w_out, g, b, *, alpha, tm=512, tf=512):
    t, d = x.shape
    f = w_out.shape[0]
    tm, tf = min(tm, t), min(tf, f)
    nf = f // tf
    return pl.pallas_call(
        functools.partial(_ffn_ln_kernel, alpha=alpha),
        grid=(t // tm, nf),
        in_specs=[
            pl.BlockSpec((tm, d), lambda i, j: (i, 0)),
            pl.BlockSpec((tm, d), lambda i, j: (i, 0)),
            pl.BlockSpec((d, tf), lambda i, j: (0, j)),
            pl.BlockSpec((d, tf), lambda i, j: (0, j + nf)),
            pl.BlockSpec((tf, d), lambda i, j: (j, 0)),
            pl.BlockSpec((1, d), lambda i, j: (0, 0)),
            pl.BlockSpec((1, d), lambda i, j: (0, 0)),
        ],
        out_specs=[pl.BlockSpec((tm, d), lambda i, j: (i, 0)), pl.BlockSpec((tm, d), lambda i, j: (i, 0))],
        out_shape=[jax.ShapeDtypeStruct((t, d), F32), jax.ShapeDtypeStruct((t, d), BF16)],
        scratch_shapes=[pltpu.VMEM((tm, d), F32)],
        compiler_params=_cparams(("parallel", "arbitrary")),
        name="ffn_ln",
    )(x, xb, w_in, w_in, w_out, g.reshape(1, d), b.reshape(1, d))


def _matmul_kernel(x_ref, w_ref, o_ref):
    o_ref[...] = _dot(x_ref[...], w_ref[...]).astype(o_ref.dtype)


def matmul(xb, w, *, out_dtype=F32, tm=512, tn=None):
    t, k = xb.shape
    n = w.shape[1]
    tm = min(tm, t)
    tn = n if tn is None else min(tn, n)
    return pl.pallas_call(
        _matmul_kernel,
        grid=(n // tn, t // tm),
        in_specs=[pl.BlockSpec((tm, k), lambda j, i: (i, 0)), pl.BlockSpec((k, tn), lambda j, i: (0, j))],
        out_specs=pl.BlockSpec((tm, tn), lambda j, i: (i, j)),
        out_shape=jax.ShapeDtypeStruct((t, n), out_dtype),
        compiler_params=_cparams(("parallel", "parallel")),
        name="matmul",
    )(xb, w)


def _proj_ln_kernel(*refs, alpha, n_parts):
    x_ref = refs[0]
    parts = refs[1:1 + n_parts]
    w_ref, g_ref, b_ref, o_ref, ob_ref = refs[1 + n_parts:]
    kp = parts[0].shape[1]
    acc = alpha * x_ref[...]
    for p in range(n_parts):
        acc = acc + _dot(parts[p][...], w_ref[p * kp:(p + 1) * kp, :])
    y = _layer_norm(acc, g_ref[...], b_ref[...])
    o_ref[...] = y
    ob_ref[...] = y.astype(BF16)


def proj_ln(x, parts, w, g, b, *, alpha, tm=512):
    t, d = x.shape
    tm = min(tm, t)
    n_parts = len(parts)
    kp = parts[0].shape[1]
    return pl.pallas_call(
        functools.partial(_proj_ln_kernel, alpha=alpha, n_parts=n_parts),
        grid=(t // tm,),
        in_specs=[pl.BlockSpec((tm, d), lambda i: (i, 0))]
        + [pl.BlockSpec((tm, kp), lambda i: (i, 0)) for _ in parts]
        + [pl.BlockSpec((n_parts * kp, d), lambda i: (0, 0)),
           pl.BlockSpec((1, d), lambda i: (0, 0)), pl.BlockSpec((1, d), lambda i: (0, 0))],
        out_specs=[pl.BlockSpec((tm, d), lambda i: (i, 0)), pl.BlockSpec((tm, d), lambda i: (i, 0))],
        out_shape=[jax.ShapeDtypeStruct((t, d), F32), jax.ShapeDtypeStruct((t, d), BF16)],
        compiler_params=_cparams(("parallel",)),
        name="proj_ln",
    )(x, *parts, w, g.reshape(1, d), b.reshape(1, d))


_LEVELS = tuple(CHUNK >> (s + 1) for s in range(int(math.log2(CHUNK)) - 1))
_UNIT_LEVEL = len(_LEVELS)
_DIAG_LEVEL = _UNIT_LEVEL + 1
GW = HEADS * LANES


def _gla_tables():
    c = CHUNK
    i = np.arange(c)[:, None]
    t = np.arange(c)[None, :]
    blocks = [t <= i, t > i]
    for h in _LEVELS:
        lower = (i // h) % 2 == 1
        blocks.append(np.where(lower, (t >= (i // h) * h) & (t <= i), (t > i) & (t <= (i // h + 1) * h - 1)))
    sums = np.concatenate(blocks, axis=0).astype(np.float32)
    msb = np.floor(np.log2(np.maximum(i ^ t, 1))).astype(np.int32)
    lvl = np.where(t < i, _UNIT_LEVEL - msb, np.where(t == i, _DIAG_LEVEL, -1)).astype(np.int32)
    return sums, lvl


def _gla_chunk(qs, ks, vs, gs, sums_ref, lvl, st_ref):
    c, dk = qs[0].shape
    e_all = _dot(sums_ref[...], jnp.concatenate([_split3(g) for g in gs], axis=-1))
    outs = []
    for h, (q, k, v, g) in enumerate(zip(qs, ks, vs, gs)):
        e = _sum3(e_all[:, 3 * dk * h:3 * dk * (h + 1)], dk)
        st = st_ref[h]
        o = _dot_nt((q * jnp.exp(e[0:c])).astype(BF16), st.astype(BF16))
        attn = jnp.where(lvl == _DIAG_LEVEL, jnp.sum(q * k, axis=-1, keepdims=True), 0.0)
        attn = jnp.where(lvl == _UNIT_LEVEL, _dot_nt((q * jnp.exp(g)).astype(BF16), k.astype(BF16)), attn)
        for l in range(len(_LEVELS)):
            x = jnp.exp(e[(2 + l) * c:(3 + l) * c])
            attn = jnp.where(lvl == l, _dot_nt((q * x).astype(BF16), (k * x).astype(BF16)), attn)
        vb = v.astype(BF16)
        outs.append(o + _dot(attn.astype(BF16), vb))
        ks_ = (k * jnp.exp(e[c:2 * c])).astype(BF16)
        st_ref[h] = st * jnp.exp(e[c - 1:c]) + _dot_tn(vb, ks_)
    return outs


def _heads(x):
    return [x[:, h * LANES:(h + 1) * LANES] for h in range(HEADS)]


def _store_heads(o_ref, sl, outs):
    for h, o in enumerate(outs):
        o_ref[sl, h * LANES:(h + 1) * LANES] = o.astype(o_ref.dtype)


def _gla_kernel(q_ref, k_ref, v_ref, r_ref, lr_ref, gw_ref, gb_ref, ng_ref, sums_ref, lvl_ref, o_ref, st_ref):
    @pl.when(pl.program_id(1) == 0)
    def _():
        st_ref[...] = jnp.zeros_like(st_ref)

    lvl = lvl_ref[...]
    ng = ng_ref[...]

    def body(ci, carry):
        sl = pl.ds(pl.multiple_of(ci * CHUNK, CHUNK), CHUNK)
        g = _log_sigmoid(_dot(lr_ref[sl, :].astype(BF16), gw_ref[...]) + gb_ref[...]) / GLA_GATE_NORM
        outs = _gla_chunk(_heads(q_ref[sl, :] * GLA_DK ** -0.5), _heads(k_ref[sl, :]), _heads(v_ref[sl, :]),
                          _heads(g), sums_ref, lvl, st_ref)
        r = _heads(_silu(r_ref[sl, :]))
        _store_heads(o_ref, sl, [_rms_norm(o, ng) * rh for o, rh in zip(outs, r)])
        return carry

    lax.fori_loop(0, q_ref.shape[0] // CHUNK, body, 0)


def _hgrn_kernel(q_ref, f_ref, v_ref, r_ref, lb_ref, ng_ref, sums_ref, lvl_ref, o_ref, st_ref):
    @pl.when(pl.program_id(1) == 0)
    def _():
        st_ref[...] = jnp.zeros_like(st_ref)

    lvl = lvl_ref[...]
    ng = ng_ref[...]
    lb = lb_ref[...]
    log_lb = jnp.log(jnp.maximum(lb, LB_EPS))
    log_1mlb = jnp.log1p(-lb)

    def body(ci, carry):
        sl = pl.ds(pl.multiple_of(ci * CHUNK, CHUNK), CHUNK)
        f_pre = f_ref[sl, :]
        a = log_lb
        b = log_1mlb + _log_sigmoid(f_pre)
        g = jnp.maximum(a, b) + jnp.log1p(jnp.exp(-jnp.abs(a - b)))
        k = (1.0 - lb) * _sigmoid(-f_pre)
        q = _silu(q_ref[sl, :]) * HGRN_DK ** -0.5
        outs = _gla_chunk(_heads(q), _heads(k), _heads(v_ref[sl, :]), _heads(g), sums_ref, lvl, st_ref)
        r = _heads(_silu(r_ref[sl, :]))
        _store_heads(o_ref, sl, [_rms_norm(o, ng) * rh for o, rh in zip(outs, r)])
        return carry

    lax.fori_loop(0, q_ref.shape[0] // CHUNK, body, 0)


def _group_block(tt, nt, idx):
    return pl.BlockSpec((tt, GW), lambda b, t: (b * nt + t, idx))


def _const_block(shape):
    return pl.BlockSpec(shape, lambda *_: tuple(0 for _ in shape))


def gla_mixer(proj, gate_w, gate_b, norm_g, *, batch, tt=512):
    t = proj.shape[0]
    s = t // batch
    tt = min(tt, s)
    nt = s // tt
    sums, lvl = _gla_tables()
    gb = functools.partial(_group_block, tt, nt)
    return pl.pallas_call(
        _gla_kernel,
        grid=(batch, nt),
        in_specs=[gb(0), gb(1), gb(2), gb(3),
                  pl.BlockSpec((tt, LANES), lambda b, t: (b * nt + t, 4 * HEADS)),
                  _const_block((LANES, GW)), _const_block((1, GW)),
                  _const_block((1, LANES)), _const_block(sums.shape), _const_block(lvl.shape)],
        out_specs=gb(0),
        out_shape=jax.ShapeDtypeStruct((t, GW), BF16),
        scratch_shapes=[pltpu.VMEM((HEADS, LANES, LANES), F32)],
        compiler_params=_cparams(("parallel", "arbitrary")),
        name="gla",
    )(proj, proj, proj, proj, proj, gate_w, gate_b, norm_g.reshape(1, LANES),
      jnp.asarray(sums, BF16), jnp.asarray(lvl))


def hgrn_mixer(proj, lower_bound, norm_g, *, batch, tt=512):
    t = proj.shape[0]
    s = t // batch
    tt = min(tt, s)
    nt = s // tt
    sums, lvl = _gla_tables()
    gb = functools.partial(_group_block, tt, nt)
    return pl.pallas_call(
        _hgrn_kernel,
        grid=(batch, nt),
        in_specs=[gb(0), gb(1), gb(2), gb(3),
                  _const_block((1, GW)),
                  _const_block((1, LANES)), _const_block(sums.shape), _const_block(lvl.shape)],
        out_specs=gb(0),
        out_shape=jax.ShapeDtypeStruct((t, GW), BF16),
        scratch_shapes=[pltpu.VMEM((HEADS, LANES, LANES), F32)],
        compiler_params=_cparams(("parallel", "arbitrary")),
        name="hgrn",
    )(proj, proj, proj, proj, lower_bound.reshape(1, GW), norm_g.reshape(1, LANES),
      jnp.asarray(sums, BF16), jnp.asarray(lvl))


def _mlstm_kernel(q_ref, k_ref, v_ref, og_ref, if_ref, cw_q_ref, cw_k_ref, gb_ref, o_ref,
                  st_ref, m_ref, qx_ref, kx_ref, qc_ref, kc_ref):
    tt = q_ref.shape[0]

    @pl.when(pl.program_id(1) == 0)
    def _():
        st_ref[...] = jnp.zeros_like(st_ref)
        m_ref[...] = jnp.zeros_like(m_ref)
        qx_ref[0:CONV_PAD, :] = jnp.zeros((CONV_PAD, GW), F32)
        kx_ref[0:CONV_PAD, :] = jnp.zeros((CONV_PAD, GW), F32)

    for src, ext, cw, dst in ((q_ref, qx_ref, cw_q_ref, qc_ref), (k_ref, kx_ref, cw_k_ref, kc_ref)):
        ext[CONV_PAD:CONV_PAD + tt, :] = src[...]
        acc = jnp.zeros((tt, GW), F32)
        for w in range(CONV_WIDTH):
            acc = acc + ext[CONV_PAD - (CONV_WIDTH - 1) + w:CONV_PAD - (CONV_WIDTH - 1) + w + tt, :] * cw[w:w + 1, :]
        dst[...] = _silu(acc)
        ext[0:CONV_PAD, :] = ext[tt:tt + CONV_PAD, :]

    c = CHUNK
    row = lax.broadcasted_iota(jnp.int32, (c, c), 0)
    col = lax.broadcasted_iota(jnp.int32, (c, c), 1)
    causal = col <= row
    eye = col == row
    tril = jnp.where(causal, 1.0, 0.0).astype(BF16)
    ones_col = jnp.where(lax.broadcasted_iota(jnp.int32, (c, LANES), 1) == 0, 1.0, 0.0)
    gbias = gb_ref[...]

    def body(ci, carry):
        sl = pl.ds(pl.multiple_of(ci * c, c), c)
        gates = if_ref[sl, :] + gbias
        cum_all = _sum3(_dot(tril, _split3(_log_sigmoid(gates))), LANES)
        qc = _heads(qc_ref[sl, :])
        kc = _heads(kc_ref[sl, :] * MLSTM_DK ** -0.5)
        vs = _heads(v_ref[sl, :])
        og = _heads(_sigmoid(og_ref[sl, :]))
        outs = []
        for h in range(HEADS):
            i_col = gates[:, h:h + 1]
            cum = cum_all[:, HEADS + h:HEADS + h + 1]
            a_col = i_col - cum
            a_row = jnp.sum(jnp.where(eye, a_col, 0.0), axis=0, keepdims=True)
            m_st = m_ref[h]
            log_intra = jnp.where(causal, cum + a_row, MASK_NEG)
            log_inter = cum + m_st
            m_t = jnp.maximum(log_inter, jnp.max(log_intra, axis=-1, keepdims=True))
            w_inter = jnp.exp(log_inter - m_t)
            w_intra = jnp.where(causal, jnp.exp(log_intra - m_t), 0.0)
            qb = qc[h].astype(BF16)
            kf = kc[h]
            v_ext = jnp.concatenate([vs[h], ones_col], axis=-1).astype(BF16)
            scores = _dot_nt(qb, kf.astype(BF16)) * w_intra
            st = st_ref[h]
            nd = w_inter * _dot(qb, st.astype(BF16)) + _dot(scores.astype(BF16), v_ext)
            den = jnp.maximum(jnp.abs(nd[:, LANES:LANES + 1]), jnp.exp(-m_t))
            outs.append(nd[:, :LANES] / den * og[h])
            cum_last = cum[c - 1:c, :]
            lli = cum_last + a_col
            log_last_inter = cum_last + m_st
            m_new = jnp.maximum(log_last_inter, jnp.max(lli, axis=0, keepdims=True))
            wk = jnp.exp(lli - m_new)
            st_ref[h] = jnp.exp(log_last_inter - m_new) * st + _dot_tn((kf * wk).astype(BF16), v_ext)
            m_ref[h] = m_new
        _store_heads(o_ref, sl, outs)
        return carry

    lax.fori_loop(0, tt // c, body, 0)


def mlstm_mixer(proj, conv_w_q, conv_w_k, gate_b, *, batch, tt=512):
    t = proj.shape[0]
    s = t // batch
    tt = min(tt, s)
    nt = s // tt
    gb = functools.partial(_group_block, tt, nt)
    return pl.pallas_call(
        _mlstm_kernel,
        grid=(batch, nt),
        in_specs=[gb(0), gb(1), gb(2), gb(3),
                  pl.BlockSpec((tt, LANES), lambda b, t: (b * nt + t, 4 * HEADS)),
                  _const_block((CONV_WIDTH, GW)), _const_block((CONV_WIDTH, GW)),
                  _const_block((1, LANES))],
        out_specs=gb(0),
        out_shape=jax.ShapeDtypeStruct((t, GW), BF16),
        scratch_shapes=[pltpu.VMEM((HEADS, LANES, 2 * LANES), F32), pltpu.VMEM((HEADS, 1, 1), F32),
                        pltpu.VMEM((CONV_PAD + tt, GW), F32), pltpu.VMEM((CONV_PAD + tt, GW), F32),
                        pltpu.VMEM((tt, GW), F32), pltpu.VMEM((tt, GW), F32)],
        compiler_params=_cparams(("parallel", "arbitrary")),
        name="mlstm",
    )(proj, proj, proj, proj, proj, conv_w_q, conv_w_k, gate_b)


LOG2E = math.log2(math.e)


def _diff_attn_kernel(sc_ref, q_ref, k_ref, vt_ref, bias_ref, ng_ref, o_ref,
                      s_ref, p_ref, mx_ref, m_ref, c_ref, a_ref):
    qi = pl.program_id(2)
    q = q_ref[...].astype(F32) * (DIFF_DK ** -0.5 * LOG2E)
    lane = lax.broadcasted_iota(jnp.int32, q.shape, 1)
    qs = (jnp.where(lane < DIFF_DK, q, 0.0).astype(BF16), jnp.where(lane >= DIFF_DK, q, 0.0).astype(BF16))
    m_ref[...] = jnp.full(m_ref.shape, MASK_NEG, F32)
    c_ref[...] = jnp.ones_like(c_ref)
    a_ref[...] = jnp.zeros_like(a_ref)
    p_ref[1] = jnp.zeros(p_ref.shape[1:], BF16)

    def scores(kt, slot):
        kk = k_ref[kt]
        tile = jnp.minimum(qi - kt, 2)
        for mp in range(2):
            s = _dot_nt(kk, qs[mp]) + bias_ref[tile]
            s_ref[slot, mp] = s
            mx_ref[slot, mp] = jnp.max(s, axis=0, keepdims=True)

    def values(kt, slot):
        vt = vt_ref[kt]
        for mp in range(2):
            a_ref[mp] = a_ref[mp] * c_ref[mp] + _dot(vt, p_ref[slot, mp])

    scores(0, 0)

    def step(ki, cur):
        nxt = 1 - cur
        scores(jnp.minimum(ki + 1, qi), nxt)
        values(jnp.maximum(ki - 1, 0), nxt)
        for mp in range(2):
            m_old = m_ref[mp]
            m_new = jnp.maximum(m_old, mx_ref[cur, mp])
            p_ref[cur, mp] = jnp.exp2(s_ref[cur, mp] - m_new).astype(BF16)
            c_ref[mp] = jnp.exp2(m_old - m_new)
            m_ref[mp] = m_new

    def pair(j, carry):
        step(2 * j, 0)
        step(2 * j + 1, 1)
        return carry

    n_tiles = qi + 1
    lax.fori_loop(0, n_tiles // 2, pair, 0)

    @pl.when(lax.rem(n_tiles, 2) == 1)
    def _():
        step(qi, 0)

    values(qi, lax.rem(qi, 2))
    lam = sc_ref[0]
    dv = o_ref.shape[1]
    out_t = (a_ref[0, 0:dv] / a_ref[0, dv:dv + 1] - lam * (a_ref[1, 0:dv] / a_ref[1, dv:dv + 1]))
    out_t = out_t * lax.rsqrt(jnp.mean(out_t * out_t, axis=0, keepdims=True) + LN_EPS)
    o_ref[...] = (out_t.T * ng_ref[...] * sc_ref[1]).astype(o_ref.dtype)


def diff_attn_mixer(q, k, vt, bias_tiles, scalars, norm_g, *, batch, tq):
    t = q.shape[0]
    nt = t // batch // tq
    ones_rows = jnp.zeros(vt.shape[:2] + (BF16_SUBLANES, tq), BF16).at[:, :, 0, :].set(1.0)
    vt = jnp.concatenate([vt, ones_rows], axis=2)
    dve = LANES + BF16_SUBLANES
    return pl.pallas_call(
        _diff_attn_kernel,
        grid=(batch, HEADS, nt),
        in_specs=[pl.BlockSpec(memory_space=pltpu.SMEM),
                  pl.BlockSpec((tq, LANES), lambda b, h, t: (b * nt + t, h)),
                  pl.BlockSpec((None, nt, tq, LANES), lambda b, h, t: (b * HEADS + h, 0, 0, 0)),
                  pl.BlockSpec((None, nt, dve, tq), lambda b, h, t: (b * HEADS + h, 0, 0, 0)),
                  pl.BlockSpec((None, 3, tq, tq), lambda b, h, t: (h, 0, 0, 0)),
                  _const_block((1, LANES))],
        out_specs=pl.BlockSpec((tq, LANES), lambda b, h, t: (b * nt + t, h)),
        out_shape=jax.ShapeDtypeStruct((t, HEADS * LANES), BF16),
        scratch_shapes=[pltpu.VMEM((2, 2, tq, tq), F32), pltpu.VMEM((2, 2, tq, tq), BF16),
                        pltpu.VMEM((2, 2, 1, tq), F32), pltpu.VMEM((2, 1, tq), F32), pltpu.VMEM((2, 1, tq), F32),
                        pltpu.VMEM((2, dve, tq), F32)],
        compiler_params=_cparams(("parallel", "parallel", "arbitrary")),
        name="diff_attn",
    )(scalars, q, k, vt, bias_tiles, norm_g.reshape(1, LANES))


def _t5_bucket_map(rel):
    n = np.maximum(rel, 0)
    max_exact = T5_BUCKETS // 2
    large = max_exact + (np.log(np.maximum(n, 1).astype(np.float32) / np.float32(max_exact))
                         / np.float32(math.log(T5_MAX_DIST / max_exact)) * (T5_BUCKETS - max_exact)).astype(np.int32)
    large = np.clip(large, max_exact, T5_BUCKETS - 1)
    return np.where(n < max_exact, n, large).astype(np.int32)


def t5_bias_tiles(t5_table, tq):
    assert tq >= T5_MAX_DIST
    j = np.arange(tq)[:, None]
    i = np.arange(tq)[None, :]
    rel = np.stack([d * tq + i - j for d in range(3)])
    bucket = jnp.asarray(_t5_bucket_map(rel).astype(np.int8))
    table = t5_table.astype(F32)
    tiles = jnp.zeros((HEADS,) + rel.shape, F32)
    for b in range(T5_BUCKETS):
        tiles = jnp.where(bucket[None] == b, table[b][:, None, None, None], tiles)
    return jnp.where(jnp.asarray(rel >= 0)[None], tiles * LOG2E, MASK_NEG)


def _xattn_kernel(xb_ref, wq_ref, k_ref, v_ref, o_ref):
    d = xb_ref.shape[1]
    hd = d // N_XHEADS
    q = _dot(xb_ref[...], wq_ref[...]).astype(BF16)
    for h in range(N_XHEADS):
        hs = slice(h * hd, (h + 1) * hd)
        s = _dot_nt(q[:, hs], k_ref[:, hs]) * hd ** -0.5
        s = s - jnp.max(s, axis=-1, keepdims=True)
        p = jnp.exp(s)
        p = p / jnp.sum(p, axis=-1, keepdims=True)
        o_ref[:, hs] = _dot(p.astype(BF16), v_ref[:, hs]).astype(o_ref.dtype)


def xattn(xb, wq, kv, *, batch, tm=512):
    t, d = xb.shape
    s = t // batch
    tm = min(tm, s)
    nt = s // tm
    n_mem = kv.shape[0] // batch
    return pl.pallas_call(
        _xattn_kernel,
        grid=(t // tm,),
        in_specs=[pl.BlockSpec((tm, d), lambda i: (i, 0)),
                  pl.BlockSpec((d, d), lambda i: (0, 0)),
                  pl.BlockSpec((n_mem, d), lambda i: (i // nt, 0)),
                  pl.BlockSpec((n_mem, d), lambda i: (i // nt, 1))],
        out_specs=pl.BlockSpec((tm, d), lambda i: (i, 0)),
        out_shape=jax.ShapeDtypeStruct((t, d), BF16),
        compiler_params=_cparams(("parallel",)),
        name="xattn",
    )(xb, wq, kv, kv)


def _pad_heads(w, dk):
    lead = w.shape[:-1]
    w = w.reshape(*lead, HEADS, dk)
    w = jnp.pad(w, [(0, 0)] * len(lead) + [(0, 0), (0, LANES - dk)])
    return w.reshape(*lead, HEADS * LANES)


def _pad_cols(w, n):
    return jnp.pad(w, [(0, 0)] * (w.ndim - 1) + [(0, n - w.shape[-1])])


def _split_in_proj(w_in):
    gw = HEADS * LANES
    sizes = (HEADS * GLA_DK, HEADS * GLA_DK, gw, GLA_GATE_RANK, gw,
             HEADS * HGRN_DK, HEADS * HGRN_DK, gw, gw,
             2 * HEADS * DIFF_DK, 2 * HEADS * DIFF_DK, gw,
             HEADS * MLSTM_DK, HEADS * MLSTM_DK, gw, 2 * HEADS, gw)
    assert sum(sizes) == w_in.shape[-1]
    (a_q, a_k, a_v, a_lr, a_r, b_q, b_f, b_i, b_g, c_q, c_k, c_v,
     d_q, d_k, d_v, d_if, d_o) = jnp.split(w_in, list(np.cumsum(sizes)[:-1]), axis=-1)
    w_gla = jnp.concatenate([_pad_heads(a_q, GLA_DK), _pad_heads(a_k, GLA_DK), a_v, a_r, _pad_cols(a_lr, LANES)], -1)
    w_hgrn = jnp.concatenate([b_q, b_f, b_i, b_g], -1)
    w_diff = jnp.concatenate([c_q, c_k, c_v], -1)
    w_mlstm = jnp.concatenate([_pad_heads(d_q, MLSTM_DK), _pad_heads(d_k, MLSTM_DK), d_v, d_o, _pad_cols(d_if, LANES)], -1)
    return [w.astype(BF16) for w in (w_gla, w_hgrn, w_diff, w_mlstm)]


def kernel(x, mem, ln_g, ln_b, ffn_w_in, ffn_w_out, w_in, w_out, gla_gate_w, gla_gate_b, gla_norm_g, hgrn_lb, hgrn_norm_g, diff_lambda, diff_norm_g, t5_table, mlstm_conv_w, mlstm_gate_b, xattn_w_q, xattn_w_kv, xattn_w_o):
    batch, seq, d = x.shape
    depth = ln_g.shape[0]
    alpha = (2 * depth) ** 0.25
    t = batch * seq

    sm = jax.nn.softmax(hgrn_lb.astype(F32), axis=0)
    lower_bounds = jnp.clip(jnp.cumsum(sm, axis=0) - sm[0], 0.0, 1.0 - 1e-6)
    tq = min(512, seq)
    bias_tiles = t5_bias_tiles(t5_table, tq)

    x = x.reshape(t, d)
    xb = x.astype(BF16)
    memb = mem.reshape(-1, d).astype(BF16)
    for l in range(depth):
        x, xb = ffn_ln(x, xb, ffn_w_in[l, 0].astype(BF16), ffn_w_out[l, 0].astype(BF16), ln_g[l, 0], ln_b[l, 0], alpha=alpha)

        w_gla, w_hgrn, w_diff, w_mlstm = _split_in_proj(w_in[l])
        o_a = gla_mixer(matmul(xb, w_gla),
                        _pad_heads(_pad_cols(gla_gate_w[l].T, LANES).T, GLA_DK).astype(BF16),
                        _pad_heads(gla_gate_b[l].reshape(1, -1), GLA_DK), gla_norm_g[l], batch=batch)
        o_b = hgrn_mixer(matmul(xb, w_hgrn), lower_bounds[l], hgrn_norm_g[l], batch=batch)
        lambda_init = 0.8 - 0.6 * math.exp(-0.3 * l)
        lq1, lk1, lq2, lk2 = [diff_lambda[l, j].astype(F32) for j in range(4)]
        lam = jnp.exp(jnp.sum(lq1 * lk1)) - jnp.exp(jnp.sum(lq2 * lk2)) + lambda_init
        p_c = matmul(xb, w_diff, out_dtype=BF16)
        gw = HEADS * LANES
        kv_tiles = lambda a: a.reshape(batch, seq // tq, tq, HEADS, LANES)
        k_c = kv_tiles(p_c[:, gw:2 * gw]).transpose(0, 3, 1, 2, 4).reshape(batch * HEADS, seq // tq, tq, LANES)
        vt_c = kv_tiles(p_c[:, 2 * gw:]).transpose(0, 3, 1, 4, 2).reshape(batch * HEADS, seq // tq, LANES, tq)
        o_c = diff_attn_mixer(p_c, k_c, vt_c, bias_tiles,
                              jnp.stack([lam, jnp.asarray(1.0 - lambda_init, F32)]).astype(F32), diff_norm_g[l], batch=batch, tq=tq)
        cw = mlstm_conv_w[l]
        o_d = mlstm_mixer(matmul(xb, w_mlstm), _pad_heads(cw[:, :HEADS * MLSTM_DK], MLSTM_DK),
                          _pad_heads(cw[:, HEADS * MLSTM_DK:], MLSTM_DK),
                          _pad_cols(mlstm_gate_b[l].reshape(1, -1), LANES), batch=batch)
        x, xb = proj_ln(x, [o_a, o_b, o_c, o_d], w_out[l].astype(BF16), ln_g[l, 1], ln_b[l, 1], alpha=alpha)

        kv = matmul(memb, xattn_w_kv[l].astype(BF16), out_dtype=BF16, tn=1024)
        att = xattn(xb, xattn_w_q[l].astype(BF16), kv, batch=batch)
        x, xb = proj_ln(x, [att], xattn_w_o[l].astype(BF16), ln_g[l, 2], ln_b[l, 2], alpha=alpha)

        x, xb = ffn_ln(x, xb, ffn_w_in[l, 1].astype(BF16), ffn_w_out[l, 1].astype(BF16), ln_g[l, 3], ln_b[l, 3], alpha=alpha)
    return x.reshape(batch, seq, d)
```

```python
import functools
import math

import numpy as np
import jax
import jax.numpy as jnp
from jax import lax
from jax.experimental import pallas as pl
from jax.experimental.pallas import tpu as pltpu

F32 = jnp.float32
BF16 = jnp.bfloat16

HEADS = 4
N_XHEADS = 4
LANES = 128
BF16_SUBLANES = 16
GLA_DK = 64
HGRN_DK = 128
DIFF_DK = 64
MLSTM_DK = 64
GLA_GATE_RANK = 16
GLA_GATE_NORM = 16.0
CONV_WIDTH = 4
CHUNK = 64
T5_BUCKETS = 32
T5_MAX_DIST = 128
LN_EPS = 1e-5
LB_EPS = 1e-12
MASK_NEG = -1e30
CONV_PAD = 8
CHUNK_UNROLL = 4

VMEM_LIMIT = 56 * 1024 * 1024


def _cparams(sem):
    return pltpu.CompilerParams(dimension_semantics=sem, vmem_limit_bytes=VMEM_LIMIT)


def _sigmoid(x):
    return 1.0 / (1.0 + jnp.exp(-x))


def _silu(x):
    return x * _sigmoid(x)


def _log_sigmoid(x):
    return jnp.minimum(x, 0.0) - jnp.log1p(jnp.exp(-jnp.abs(x)))


def _layer_norm(y, g, b):
    mu = jnp.mean(y, axis=-1, keepdims=True)
    d = y - mu
    var = jnp.mean(d * d, axis=-1, keepdims=True)
    return d * lax.rsqrt(var + LN_EPS) * g + b


def _rms_norm(y, g):
    return y * lax.rsqrt(jnp.mean(y * y, axis=-1, keepdims=True) + LN_EPS) * g


def _dot(a, b):
    return jnp.dot(a, b, preferred_element_type=F32)


def _dot_nt(a, b):
    return lax.dot_general(a, b, (((1,), (1,)), ((), ())), preferred_element_type=F32)


def _dot_tn(a, b):
    return lax.dot_general(a, b, (((0,), (0,)), ((), ())), preferred_element_type=F32)


def _split3(g):
    hi = g.astype(BF16)
    r1 = g - hi.astype(F32)
    mid = r1.astype(BF16)
    lo = (r1 - mid.astype(F32)).astype(BF16)
    return jnp.concatenate([hi, mid, lo], axis=0)


def _ffn_ln_kernel(x_ref, xb_ref, wg_ref, wu_ref, wo_ref, g_ref, b_ref, o_ref, ob_ref, acc_ref, *, alpha):
    j = pl.program_id(1)

    @pl.when(j == 0)
    def _():
        acc_ref[...] = jnp.zeros_like(acc_ref)

    xb = xb_ref[...]
    gate = _dot(xb, wg_ref[...])
    up = _dot(xb, wu_ref[...])
    h = (_silu(gate) * up).astype(BF16)
    acc_ref[...] += _dot(h, wo_ref[...])

    @pl.when(j == pl.num_programs(1) - 1)
    def _():
        y = _layer_norm(alpha * x_ref[...] + 0.5 * acc_ref[...], g_ref[...], b_ref[...])
        o_ref[...] = y
        ob_ref[...] = y.astype(BF16)


def ffn_ln(x, xb, w_in, w_out, g, b, *, alpha, tm=512, tf=512):
    t, d = x.shape
    f = w_out.shape[0]
    tm, tf = min(tm, t), min(tf, f)
    nf = f // tf
    return pl.pallas_call(
        functools.partial(_ffn_ln_kernel, alpha=alpha),
        grid=(t // tm, nf),
        in_specs=[
            pl.BlockSpec((tm, d), lambda i, j: (i, 0)),
            pl.BlockSpec((tm, d), lambda i, j: (i, 0)),
            pl.BlockSpec((d, tf), lambda i, j: (0, j)),
            pl.BlockSpec((d, tf), lambda i, j: (0, j + nf)),
            pl.BlockSpec((tf, d), lambda i, j: (j, 0)),
            pl.BlockSpec((1, d), lambda i, j: (0, 0)),
            pl.BlockSpec((1, d), lambda i, j: (0, 0)),
        ],
        out_specs=[pl.BlockSpec((tm, d), lambda i, j: (i, 0)), pl.BlockSpec((tm, d), lambda i, j: (i, 0))],
        out_shape=[jax.ShapeDtypeStruct((t, d), F32), jax.ShapeDtypeStruct((t, d), BF16)],
        scratch_shapes=[pltpu.VMEM((tm, d), F32)],
        compiler_params=_cparams(("parallel", "arbitrary")),
        name="ffn_ln",
    )(x, xb, w_in, w_in, w_out, g.reshape(1, d), b.reshape(1, d))


def _matmul_kernel(x_ref, w_ref, o_ref):
    o_ref[...] = _dot(x_ref[...], w_ref[...]).astype(o_ref.dtype)


def matmul(xb, w, *, out_dtype=F32, tm=512, tn=None):
    t, k = xb.shape
    n = w.shape[1]
    tm = min(tm, t)
    tn = n if tn is None else min(tn, n)
    return pl.pallas_call(
        _matmul_kernel,
        grid=(n // tn, t // tm),
        in_specs=[pl.BlockSpec((tm, k), lambda j, i: (i, 0)), pl.BlockSpec((k, tn), lambda j, i: (0, j))],
        out_specs=pl.BlockSpec((tm, tn), lambda j, i: (i, j)),
        out_shape=jax.ShapeDtypeStruct((t, n), out_dtype),
        compiler_params=_cparams(("parallel", "parallel")),
        name="matmul",
    )(xb, w)


def _proj_ln_kernel(*refs, alpha, n_parts):
    x_ref = refs[0]
    parts = refs[1:1 + n_parts]
    w_ref, g_ref, b_ref, o_ref, ob_ref = refs[1 + n_parts:]
    kp = parts[0].shape[1]
    acc = alpha * x_ref[...]
    for p in range(n_parts):
        acc = acc + _dot(parts[p][...], w_ref[p * kp:(p + 1) * kp, :])
    y = _layer_norm(acc, g_ref[...], b_ref[...])
    o_ref[...] = y
    ob_ref[...] = y.astype(BF16)


def proj_ln(x, parts, w, g, b, *, alpha, tm=512):
    t, d = x.shape
    tm = min(tm, t)
    n_parts = len(parts)
    kp = parts[0].shape[1]
    return pl.pallas_call(
        functools.partial(_proj_ln_kernel, alpha=alpha, n_parts=n_parts),
        grid=(t // tm,),
        in_specs=[pl.BlockSpec((tm, d), lambda i: (i, 0))]
        + [pl.BlockSpec((tm, kp), lambda i: (i, 0)) for _ in parts]
        + [pl.BlockSpec((n_parts * kp, d), lambda i: (0, 0)),
           pl.BlockSpec((1, d), lambda i: (0, 0)), pl.BlockSpec((1, d), lambda i: (0, 0))],
        out_specs=[pl.BlockSpec((tm, d), lambda i: (i, 0)), pl.BlockSpec((tm, d), lambda i: (i, 0))],
        out_shape=[jax.ShapeDtypeStruct((t, d), F32), jax.ShapeDtypeStruct((t, d), BF16)],
        compiler_params=_cparams(("parallel",)),
        name="proj_ln",
    )(x, *parts, w, g.reshape(1, d), b.reshape(1, d))


_LEVELS = tuple(CHUNK >> (s + 1) for s in range(int(math.log2(CHUNK)) - 1))
_UNIT_LEVEL = len(_LEVELS)
_DIAG_LEVEL = _UNIT_LEVEL + 1
GW = HEADS * LANES


def _gla_tables():
    c = CHUNK
    i = np.arange(c)[:, None]
    t = np.arange(c)[None, :]
    blocks = [t <= i, t > i]
    for h in _LEVELS:
        lower = (i // h) % 2 == 1
        blocks.append(np.where(lower, (t >= (i // h) * h) & (t <= i), (t > i) & (t <= (i // h + 1) * h - 1)))
    sums = np.concatenate(blocks, axis=0).astype(np.float32)
    sums = np.concatenate([sums] * 3, axis=1)
    msb = np.floor(np.log2(np.maximum(i ^ t, 1))).astype(np.int32)
    lvl = np.where(t < i, _UNIT_LEVEL - msb, np.where(t == i, _DIAG_LEVEL, -1)).astype(np.int32)
    return sums, lvl


def _gla_chunk(qs, ks, vs, gs, sums_ref, lvl, st_ref):
    c, dk = qs[0].shape
    e_all = _dot(sums_ref[...], _split3(jnp.concatenate(gs, axis=-1)))
    outs = []
    for h, (q, k, v, g) in enumerate(zip(qs, ks, vs, gs)):
        e = e_all[:, dk * h:dk * (h + 1)]
        st = st_ref[h]
        o = _dot_nt((q * jnp.exp(e[0:c])).astype(BF16), st.astype(BF16))
        attn = jnp.where(lvl == _DIAG_LEVEL, jnp.sum(q * k, axis=-1, keepdims=True), 0.0)
        attn = jnp.where(lvl == _UNIT_LEVEL, _dot_nt((q * jnp.exp(g)).astype(BF16), k.astype(BF16)), attn)
        for l in range(len(_LEVELS)):
            x = jnp.exp(e[(2 + l) * c:(3 + l) * c])
            attn = jnp.where(lvl == l, _dot_nt((q * x).astype(BF16), (k * x).astype(BF16)), attn)
        vb = v.astype(BF16)
        outs.append(o + _dot(attn.astype(BF16), vb))
        ks_ = (k * jnp.exp(e[c:2 * c])).astype(BF16)
        st_ref[h] = st * jnp.exp(e[c - 1:c]) + _dot_tn(vb, ks_)
    return outs


def _heads(x):
    return [x[:, h * LANES:(h + 1) * LANES] for h in range(HEADS)]


def _store_heads(o_ref, sl, outs):
    for h, o in enumerate(outs):
        o_ref[sl, h * LANES:(h + 1) * LANES] = o.astype(o_ref.dtype)


def _gla_kernel(q_ref, k_ref, v_ref, r_ref, lr_ref, gw_ref, gb_ref, ng_ref, sums_ref, lvl_ref, o_ref, st_ref):
    @pl.when(pl.program_id(1) == 0)
    def _():
        st_ref[...] = jnp.zeros_like(st_ref)

    lvl = lvl_ref[...]
    ng = ng_ref[...]

    def body(ci, carry):
        sl = pl.ds(pl.multiple_of(ci * CHUNK, CHUNK), CHUNK)
        g = _log_sigmoid(_dot(lr_ref[sl, :].astype(BF16), gw_ref[...]) + gb_ref[...]) / GLA_GATE_NORM
        outs = _gla_chunk(_heads(q_ref[sl, :] * GLA_DK ** -0.5), _heads(k_ref[sl, :]), _heads(v_ref[sl, :]),
                          _heads(g), sums_ref, lvl, st_ref)
        r = _heads(_silu(r_ref[sl, :]))
        _store_heads(o_ref, sl, [_rms_norm(o, ng) * rh for o, rh in zip(outs, r)])
        return carry

    lax.fori_loop(0, q_ref.shape[0] // CHUNK, body, 0, unroll=CHUNK_UNROLL)


def _hgrn_kernel(q_ref, f_ref, v_ref, r_ref, lb_ref, ng_ref, sums_ref, lvl_ref, o_ref, st_ref):
    @pl.when(pl.program_id(1) == 0)
    def _():
        st_ref[...] = jnp.zeros_like(st_ref)

    lvl = lvl_ref[...]
    ng = ng_ref[...]
    lb = lb_ref[...]
    lb_floor = jnp.maximum(lb, LB_EPS)

    def body(ci, carry):
        sl = pl.ds(pl.multiple_of(ci * CHUNK, CHUNK), CHUNK)
        f_pre = f_ref[sl, :]
        t = jnp.exp(-jnp.abs(f_pre))
        r = 1.0 / (1.0 + t)
        sig_pos = jnp.where(f_pre >= 0, r, t * r)
        sig_neg = jnp.where(f_pre >= 0, t * r, r)
        g = jnp.log(lb_floor + (1.0 - lb) * sig_pos)
        k = (1.0 - lb) * sig_neg
        q = _silu(q_ref[sl, :]) * HGRN_DK ** -0.5
        outs = _gla_chunk(_heads(q), _heads(k), _heads(v_ref[sl, :]), _heads(g), sums_ref, lvl, st_ref)
        r = _heads(_silu(r_ref[sl, :]))
        _store_heads(o_ref, sl, [_rms_norm(o, ng) * rh for o, rh in zip(outs, r)])
        return carry

    lax.fori_loop(0, q_ref.shape[0] // CHUNK, body, 0, unroll=CHUNK_UNROLL)


def _group_block(tt, nt, idx):
    return pl.BlockSpec((tt, GW), lambda b, t: (b * nt + t, idx))


def _const_block(shape):
    return pl.BlockSpec(shape, lambda *_: tuple(0 for _ in shape))


def gla_mixer(proj, gate_w, gate_b, norm_g, *, batch, tt=512):
    t = proj.shape[0]
    s = t // batch
    tt = min(tt, s)
    nt = s // tt
    sums, lvl = _gla_tables()
    gb = functools.partial(_group_block, tt, nt)
    return pl.pallas_call(
        _gla_kernel,
        grid=(batch, nt),
        in_specs=[gb(0), gb(1), gb(2), gb(3),
                  pl.BlockSpec((tt, LANES), lambda b, t: (b * nt + t, 4 * HEADS)),
                  _const_block((LANES, GW)), _const_block((1, GW)),
                  _const_block((1, LANES)), _const_block(sums.shape), _const_block(lvl.shape)],
        out_specs=gb(0),
        out_shape=jax.ShapeDtypeStruct((t, GW), BF16),
        scratch_shapes=[pltpu.VMEM((HEADS, LANES, LANES), F32)],
        compiler_params=_cparams(("parallel", "arbitrary")),
        name="gla",
    )(proj, proj, proj, proj, proj, gate_w, gate_b, norm_g.reshape(1, LANES),
      jnp.asarray(sums, BF16), jnp.asarray(lvl))


def hgrn_mixer(proj, lower_bound, norm_g, *, batch, tt=512):
    t = proj.shape[0]
    s = t // batch
    tt = min(tt, s)
    nt = s // tt
    sums, lvl = _gla_tables()
    gb = functools.partial(_group_block, tt, nt)
    return pl.pallas_call(
        _hgrn_kernel,
        grid=(batch, nt),
        in_specs=[gb(0), gb(1), gb(2), gb(3),
                  _const_block((1, GW)),
                  _const_block((1, LANES)), _const_block(sums.shape), _const_block(lvl.shape)],
        out_specs=gb(0),
        out_shape=jax.ShapeDtypeStruct((t, GW), BF16),
        scratch_shapes=[pltpu.VMEM((HEADS, LANES, LANES), F32)],
        compiler_params=_cparams(("parallel", "arbitrary")),
        name="hgrn",
    )(proj, proj, proj, proj, lower_bound.reshape(1, GW), norm_g.reshape(1, LANES),
      jnp.asarray(sums, BF16), jnp.asarray(lvl))


def _mlstm_kernel(q_ref, k_ref, v_ref, og_ref, if_ref, cw_q_ref, cw_k_ref, gb_ref, o_ref,
                  st_ref, m_ref, qx_ref, kx_ref):
    tt = q_ref.shape[0]

    @pl.when(pl.program_id(1) == 0)
    def _():
        st_ref[...] = jnp.zeros_like(st_ref)
        m_ref[...] = jnp.zeros_like(m_ref)
        qx_ref[0:CONV_PAD, :] = jnp.zeros((CONV_PAD, GW), F32)
        kx_ref[0:CONV_PAD, :] = jnp.zeros((CONV_PAD, GW), F32)

    qx_ref[CONV_PAD:CONV_PAD + tt, :] = q_ref[...]
    kx_ref[CONV_PAD:CONV_PAD + tt, :] = k_ref[...]

    c = CHUNK

    def conv_silu(ext, cw, r0):
        xw = ext[pl.ds(r0, c + CONV_PAD), :]
        acc = xw[CONV_PAD - (CONV_WIDTH - 1):CONV_PAD - (CONV_WIDTH - 1) + c] * cw[0:1, :]
        for w in range(1, CONV_WIDTH):
            acc = acc + xw[CONV_PAD - (CONV_WIDTH - 1) + w:CONV_PAD - (CONV_WIDTH - 1) + w + c] * cw[w:w + 1, :]
        return _silu(acc)

    row = lax.broadcasted_iota(jnp.int32, (c, c), 0)
    col = lax.broadcasted_iota(jnp.int32, (c, c), 1)
    causal = col <= row
    eye = col == row
    tril3 = jnp.concatenate([jnp.where(causal, 1.0, 0.0).astype(BF16)] * 3, axis=1)
    ones_col = jnp.where(lax.broadcasted_iota(jnp.int32, (c, LANES), 1) == 0, 1.0, 0.0)
    gbias = gb_ref[...]

    def body(ci, carry):
        r0 = pl.multiple_of(ci * c, c)
        sl = pl.ds(r0, c)
        gates = if_ref[sl, :] + gbias
        cum_all = _dot(tril3, _split3(_log_sigmoid(gates)))
        qc = _heads(conv_silu(qx_ref, cw_q_ref, r0))
        kc = _heads(conv_silu(kx_ref, cw_k_ref, r0) * MLSTM_DK ** -0.5)
        vs = _heads(v_ref[sl, :])
        og = _heads(_sigmoid(og_ref[sl, :]))
        outs = []
        for h in range(HEADS):
            i_col = gates[:, h:h + 1]
            cum = cum_all[:, HEADS + h:HEADS + h + 1]
            a_col = i_col - cum
            a_row = jnp.sum(jnp.where(eye, a_col, 0.0), axis=0, keepdims=True)
            m_st = m_ref[h]
            log_intra = jnp.where(causal, cum + a_row, MASK_NEG)
            log_inter = cum + m_st
            m_t = jnp.maximum(log_inter, jnp.max(log_intra, axis=-1, keepdims=True))
            w_inter = jnp.exp(log_inter - m_t)
            w_intra = jnp.where(causal, jnp.exp(log_intra - m_t), 0.0)
            qb = qc[h].astype(BF16)
            kf = kc[h]
            v_ext = jnp.concatenate([vs[h], ones_col], axis=-1).astype(BF16)
            scores = _dot_nt(qb, kf.astype(BF16)) * w_intra
            st = st_ref[h]
            nd = w_inter * _dot(qb, st.astype(BF16)) + _dot(scores.astype(BF16), v_ext)
            den = jnp.maximum(jnp.abs(nd[:, LANES:LANES + 1]), jnp.exp(-m_t))
            outs.append(nd[:, :LANES] / den * og[h])
            cum_last = cum[c - 1:c, :]
            lli = cum_last + a_col
            log_last_inter = cum_last + m_st
            m_new = jnp.maximum(log_last_inter, jnp.max(lli, axis=0, keepdims=True))
            wk = jnp.exp(lli - m_new)
            st_ref[h] = jnp.exp(log_last_inter - m_new) * st + _dot_tn((kf * wk).astype(BF16), v_ext)
            m_ref[h] = m_new
        _store_heads(o_ref, sl, outs)
        return carry

    lax.fori_loop(0, tt // c, body, 0, unroll=CHUNK_UNROLL)
    qx_ref[0:CONV_PAD, :] = qx_ref[tt:tt + CONV_PAD, :]
    kx_ref[0:CONV_PAD, :] = kx_ref[tt:tt + CONV_PAD, :]


def mlstm_mixer(proj, conv_w_q, conv_w_k, gate_b, *, batch, tt=512):
    t = proj.shape[0]
    s = t // batch
    tt = min(tt, s)
    nt = s // tt
    gb = functools.partial(_group_block, tt, nt)
    return pl.pallas_call(
        _mlstm_kernel,
        grid=(batch, nt),
        in_specs=[gb(0), gb(1), gb(2), gb(3),
                  pl.BlockSpec((tt, LANES), lambda b, t: (b * nt + t, 4 * HEADS)),
                  _const_block((CONV_WIDTH, GW)), _const_block((CONV_WIDTH, GW)),
                  _const_block((1, LANES))],
        out_specs=gb(0),
        out_shape=jax.ShapeDtypeStruct((t, GW), BF16),
        scratch_shapes=[pltpu.VMEM((HEADS, LANES, 2 * LANES), F32), pltpu.VMEM((HEADS, 1, 1), F32),
                        pltpu.VMEM((CONV_PAD + tt, GW), F32), pltpu.VMEM((CONV_PAD + tt, GW), F32)],
        compiler_params=_cparams(("parallel", "arbitrary")),
        name="mlstm",
    )(proj, proj, proj, proj, proj, conv_w_q, conv_w_k, gate_b)


LOG2E = math.log2(math.e)


def _diff_attn_kernel(sc_ref, q_ref, k_ref, vt_ref, bias_ref, ng_ref, o_ref,
                      s_ref, p_ref, mx_ref, m_ref, c_ref, a_ref):
    qi = pl.program_id(2)
    q = q_ref[...].astype(F32) * (DIFF_DK ** -0.5 * LOG2E)
    lane = lax.broadcasted_iota(jnp.int32, q.shape, 1)
    qs = (jnp.where(lane < DIFF_DK, q, 0.0).astype(BF16), jnp.where(lane >= DIFF_DK, q, 0.0).astype(BF16))
    m_ref[...] = jnp.full(m_ref.shape, MASK_NEG, F32)
    c_ref[...] = jnp.ones_like(c_ref)
    a_ref[...] = jnp.zeros_like(a_ref)
    p_ref[1] = jnp.zeros(p_ref.shape[1:], BF16)
    tk = vt_ref.shape[2]
    ones_rows = jnp.where(lax.broadcasted_iota(jnp.int32, (BF16_SUBLANES, tk), 0) == 0, 1.0, 0.0).astype(BF16)

    def scores(kt, slot):
        kk = k_ref[pl.ds(pl.multiple_of(kt * tk, tk), tk), :]
        tile = jnp.minimum(qi - kt, 2)
        for mp in range(2):
            s = _dot_nt(kk, qs[mp]) + bias_ref[tile]
            s_ref[slot, mp] = s
            mx_ref[slot, mp] = jnp.max(s, axis=0, keepdims=True)

    def values(kt, slot):
        vt = jnp.concatenate([vt_ref[kt], ones_rows], axis=0)
        for mp in range(2):
            a_ref[mp] = a_ref[mp] * c_ref[mp] + _dot(vt, p_ref[slot, mp])

    scores(0, 0)

    def step(ki, cur):
        nxt = 1 - cur
        scores(jnp.minimum(ki + 1, qi), nxt)
        values(jnp.maximum(ki - 1, 0), nxt)
        for mp in range(2):
            m_old = m_ref[mp]
            m_new = jnp.maximum(m_old, mx_ref[cur, mp])
            p_ref[cur, mp] = jnp.exp2(s_ref[cur, mp] - m_new).astype(BF16)
            c_ref[mp] = jnp.exp2(m_old - m_new)
            m_ref[mp] = m_new

    def pair(j, carry):
        step(2 * j, 0)
        step(2 * j + 1, 1)
        return carry

    n_tiles = qi + 1
    lax.fori_loop(0, n_tiles // 2, pair, 0)

    @pl.when(lax.rem(n_tiles, 2) == 1)
    def _():
        step(qi, 0)

    values(qi, lax.rem(qi, 2))
    lam = sc_ref[0]
    dv = o_ref.shape[1]
    out_t = (a_ref[0, 0:dv] / a_ref[0, dv:dv + 1] - lam * (a_ref[1, 0:dv] / a_ref[1, dv:dv + 1]))
    out_t = out_t * lax.rsqrt(jnp.mean(out_t * out_t, axis=0, keepdims=True) + LN_EPS)
    o_ref[...] = (out_t.T * ng_ref[...] * sc_ref[1]).astype(o_ref.dtype)


def diff_attn_mixer(qk, vt, bias_tiles, scalars, norm_g, *, batch, tq):
    t = qk.shape[0]
    s = t // batch
    nt = s // tq
    dve = LANES + BF16_SUBLANES
    return pl.pallas_call(
        _diff_attn_kernel,
        grid=(batch, HEADS, nt),
        in_specs=[pl.BlockSpec(memory_space=pltpu.SMEM),
                  pl.BlockSpec((tq, LANES), lambda b, h, t: (b * nt + t, h)),
                  pl.BlockSpec((s, LANES), lambda b, h, t: (b, HEADS + h)),
                  pl.BlockSpec((None, nt, LANES, tq), lambda b, h, t: (b * HEADS + h, 0, 0, 0)),
                  pl.BlockSpec((None, 3, tq, tq), lambda b, h, t: (h, 0, 0, 0)),
                  _const_block((1, LANES))],
        out_specs=pl.BlockSpec((tq, LANES), lambda b, h, t: (b * nt + t, h)),
        out_shape=jax.ShapeDtypeStruct((t, HEADS * LANES), BF16),
        scratch_shapes=[pltpu.VMEM((2, 2, tq, tq), F32), pltpu.VMEM((2, 2, tq, tq), BF16),
                        pltpu.VMEM((2, 2, 1, tq), F32), pltpu.VMEM((2, 1, tq), F32), pltpu.VMEM((2, 1, tq), F32),
                        pltpu.VMEM((2, dve, tq), F32)],
        compiler_params=_cparams(("parallel", "parallel", "arbitrary")),
        name="diff_attn",
    )(scalars, qk, qk, vt, bias_tiles, norm_g.reshape(1, LANES))


def _t5_bucket_map(rel):
    n = np.maximum(rel, 0)
    max_exact = T5_BUCKETS // 2
    large = max_exact + (np.log(np.maximum(n, 1).astype(np.float32) / np.float32(max_exact))
                         / np.float32(math.log(T5_MAX_DIST / max_exact)) * (T5_BUCKETS - max_exact)).astype(np.int32)
    large = np.clip(large, max_exact, T5_BUCKETS - 1)
    return np.where(n < max_exact, n, large).astype(np.int32)


def t5_bias_tiles(t5_table, tq):
    assert tq >= T5_MAX_DIST
    j = np.arange(tq)[:, None]
    i = np.arange(tq)[None, :]
    rel = np.stack([d * tq + i - j for d in range(3)])
    bucket = jnp.asarray(_t5_bucket_map(rel).astype(np.int8))
    table = t5_table.astype(F32)
    tiles = jnp.zeros((HEADS,) + rel.shape, F32)
    for b in range(T5_BUCKETS):
        tiles = jnp.where(bucket[None] == b, table[b][:, None, None, None], tiles)
    return jnp.where(jnp.asarray(rel >= 0)[None], tiles * LOG2E, MASK_NEG)


def _xattn_kernel(xb_ref, wq_ref, k_ref, v_ref, o_ref):
    d = xb_ref.shape[1]
    hd = d // N_XHEADS
    q = _dot(xb_ref[...], wq_ref[...]).astype(BF16)
    for h in range(N_XHEADS):
        hs = slice(h * hd, (h + 1) * hd)
        s = _dot_nt(q[:, hs], k_ref[:, hs]) * hd ** -0.5
        s = s - jnp.max(s, axis=-1, keepdims=True)
        p = jnp.exp(s)
        p = p / jnp.sum(p, axis=-1, keepdims=True)
        o_ref[:, hs] = _dot(p.astype(BF16), v_ref[:, hs]).astype(o_ref.dtype)


def xattn(xb, wq, kv, *, batch, tm=512):
    t, d = xb.shape
    s = t // batch
    tm = min(tm, s)
    nt = s // tm
    n_mem = kv.shape[0] // batch
    return pl.pallas_call(
        _xattn_kernel,
        grid=(t // tm,),
        in_specs=[pl.BlockSpec((tm, d), lambda i: (i, 0)),
                  pl.BlockSpec((d, d), lambda i: (0, 0)),
                  pl.BlockSpec((n_mem, d), lambda i: (i // nt, 0)),
                  pl.BlockSpec((n_mem, d), lambda i: (i // nt, 1))],
        out_specs=pl.BlockSpec((tm, d), lambda i: (i, 0)),
        out_shape=jax.ShapeDtypeStruct((t, d), BF16),
        compiler_params=_cparams(("parallel",)),
        name="xattn",
    )(xb, wq, kv, kv)


def _pad_heads(w, dk):
    lead = w.shape[:-1]
    w = w.reshape(*lead, HEADS, dk)
    w = jnp.pad(w, [(0, 0)] * len(lead) + [(0, 0), (0, LANES - dk)])
    return w.reshape(*lead, HEADS * LANES)


def _pad_cols(w, n):
    return jnp.pad(w, [(0, 0)] * (w.ndim - 1) + [(0, n - w.shape[-1])])


def _split_in_proj(w_in):
    gw = HEADS * LANES
    sizes = (HEADS * GLA_DK, HEADS * GLA_DK, gw, GLA_GATE_RANK, gw,
             HEADS * HGRN_DK, HEADS * HGRN_DK, gw, gw,
             2 * HEADS * DIFF_DK, 2 * HEADS * DIFF_DK, gw,
             HEADS * MLSTM_DK, HEADS * MLSTM_DK, gw, 2 * HEADS, gw)
    assert sum(sizes) == w_in.shape[-1]
    (a_q, a_k, a_v, a_lr, a_r, b_q, b_f, b_i, b_g, c_q, c_k, c_v,
     d_q, d_k, d_v, d_if, d_o) = jnp.split(w_in, list(np.cumsum(sizes)[:-1]), axis=-1)
    w_gla = jnp.concatenate([_pad_heads(a_q, GLA_DK), _pad_heads(a_k, GLA_DK), a_v, a_r, _pad_cols(a_lr, LANES)], -1)
    w_hgrn = jnp.concatenate([b_q, b_f, b_i, b_g], -1)
    w_diff = jnp.concatenate([c_q, c_k, c_v], -1)
    w_mlstm = jnp.concatenate([_pad_heads(d_q, MLSTM_DK), _pad_heads(d_k, MLSTM_DK), d_v, d_o, _pad_cols(d_if, LANES)], -1)
    return [w.astype(BF16) for w in (w_gla, w_hgrn, w_diff, w_mlstm)]


def kernel(x, mem, ln_g, ln_b, ffn_w_in, ffn_w_out, w_in, w_out, gla_gate_w, gla_gate_b, gla_norm_g, hgrn_lb, hgrn_norm_g, diff_lambda, diff_norm_g, t5_table, mlstm_conv_w, mlstm_gate_b, xattn_w_q, xattn_w_kv, xattn_w_o):
    batch, seq, d = x.shape
    depth = ln_g.shape[0]
    alpha = (2 * depth) ** 0.25
    t = batch * seq

    sm = jax.nn.softmax(hgrn_lb.astype(F32), axis=0)
    lower_bounds = jnp.clip(jnp.cumsum(sm, axis=0) - sm[0], 0.0, 1.0 - 1e-6)
    tq = min(512, seq)
    bias_tiles = t5_bias_tiles(t5_table, tq)

    x = x.reshape(t, d)
    xb = x.astype(BF16)
    memb = mem.reshape(-1, d).astype(BF16)
    for l in range(depth):
        x, xb = ffn_ln(x, xb, ffn_w_in[l, 0].astype(BF16), ffn_w_out[l, 0].astype(BF16), ln_g[l, 0], ln_b[l, 0], alpha=alpha)

        w_gla, w_hgrn, w_diff, w_mlstm = _split_in_proj(w_in[l])
        o_a = gla_mixer(matmul(xb, w_gla),
                        _pad_heads(_pad_cols(gla_gate_w[l].T, LANES).T, GLA_DK).astype(BF16),
                        _pad_heads(gla_gate_b[l].reshape(1, -1), GLA_DK), gla_norm_g[l], batch=batch)
        o_b = hgrn_mixer(matmul(xb, w_hgrn), lower_bounds[l], hgrn_norm_g[l], batch=batch)
        lambda_init = 0.8 - 0.6 * math.exp(-0.3 * l)
        lq1, lk1, lq2, lk2 = [diff_lambda[l, j].astype(F32) for j in range(4)]
        lam = jnp.exp(jnp.sum(lq1 * lk1)) - jnp.exp(jnp.sum(lq2 * lk2)) + lambda_init
        p_c = matmul(xb, w_diff, out_dtype=BF16)
        vt_c = (p_c[:, 2 * GW:].reshape(batch, seq // tq, tq, HEADS, LANES).transpose(0, 3, 1, 4, 2)
                .reshape(batch * HEADS, seq // tq, LANES, tq))
        o_c = diff_attn_mixer(p_c, vt_c, bias_tiles,
                              jnp.stack([lam, jnp.asarray(1.0 - lambda_init, F32)]).astype(F32), diff_norm_g[l], batch=batch, tq=tq)
        cw = mlstm_conv_w[l]
        o_d = mlstm_mixer(matmul(xb, w_mlstm), _pad_heads(cw[:, :HEADS * MLSTM_DK], MLSTM_DK),
                          _pad_heads(cw[:, HEADS * MLSTM_DK:], MLSTM_DK),
                          _pad_cols(mlstm_gate_b[l].reshape(1, -1), LANES), batch=batch)
        x, xb = proj_ln(x, [o_a, o_b, o_c, o_d], w_out[l].astype(BF16), ln_g[l, 1], ln_b[l, 1], alpha=alpha)

        kv = matmul(memb, xattn_w_kv[l].astype(BF16), out_dtype=BF16, tn=1024)
        att = xattn(xb, xattn_w_q[l].astype(BF16), kv, batch=batch)
        x, xb = proj_ln(x, [att], xattn_w_o[l].astype(BF16), ln_g[l, 2], ln_b[l, 2], alpha=alpha)

        x, xb = ffn_ln(x, xb, ffn_w_in[l, 1].astype(BF16), ffn_w_out[l, 1].astype(BF16), ln_g[l, 3], ln_b[l, 3], alpha=alpha)
    return x.reshape(batch, seq, d)
```

```python
import functools
import math

import numpy as np
import jax
import jax.numpy as jnp
from jax import lax
from jax.experimental import pallas as pl
from jax.experimental.pallas import tpu as pltpu

F32 = jnp.float32
BF16 = jnp.bfloat16

HEADS = 4
N_XHEADS = 4
LANES = 128
BF16_SUBLANES = 16
GLA_DK = 64
HGRN_DK = 128
DIFF_DK = 64
MLSTM_DK = 64
GLA_GATE_RANK = 16
GLA_GATE_NORM = 16.0
CONV_WIDTH = 4
CHUNK = 64
T5_BUCKETS = 32
T5_MAX_DIST = 128
LN_EPS = 1e-5
LB_EPS = 1e-12
MASK_NEG = -1e30
CONV_PAD = 8
CHUNK_UNROLL = 4

VMEM_LIMIT = 56 * 1024 * 1024


def _cparams(sem):
    return pltpu.CompilerParams(dimension_semantics=sem, vmem_limit_bytes=VMEM_LIMIT)


def _sigmoid(x):
    return 1.0 / (1.0 + jnp.exp(-x))


def _silu(x):
    return x * _sigmoid(x)


def _log_sigmoid(x):
    return jnp.minimum(x, 0.0) - jnp.log1p(jnp.exp(-jnp.abs(x)))


def _layer_norm(y, g, b):
    mu = jnp.mean(y, axis=-1, keepdims=True)
    d = y - mu
    var = jnp.mean(d * d, axis=-1, keepdims=True)
    return d * lax.rsqrt(var + LN_EPS) * g + b


def _rms_norm(y, g):
    return y * lax.rsqrt(jnp.mean(y * y, axis=-1, keepdims=True) + LN_EPS) * g


def _dot(a, b):
    return jnp.dot(a, b, preferred_element_type=F32)


def _dot_nt(a, b):
    return lax.dot_general(a, b, (((1,), (1,)), ((), ())), preferred_element_type=F32)


def _dot_tn(a, b):
    return lax.dot_general(a, b, (((0,), (0,)), ((), ())), preferred_element_type=F32)


def _split3(g, axis=0):
    hi = g.astype(BF16)
    r1 = g - hi.astype(F32)
    mid = r1.astype(BF16)
    lo = (r1 - mid.astype(F32)).astype(BF16)
    return jnp.concatenate([hi, mid, lo], axis=axis)


ROW_SPLIT = 2
FFN_TILE = 512


def _ffn_ln_kernel(x_ref, xb_ref, wgu_hbm, wo_hbm, g_ref, b_ref, o_ref, ob_ref,
                   acc_ref, wgu0_ref, wo0_ref, wgu_buf, wo_buf, sem0, sems, *, alpha):
    nf = wgu_hbm.shape[0]
    tm = x_ref.shape[0]
    tf = wo0_ref.shape[0]

    def tile_copies(j, slot):
        return (pltpu.make_async_copy(wgu_hbm.at[j], wgu_buf.at[slot], sems.at[0, slot]),
                pltpu.make_async_copy(wo_hbm.at[j], wo_buf.at[slot], sems.at[1, slot]))

    @pl.when(pl.program_id(0) == 0)
    def _():
        first = (pltpu.make_async_copy(wgu_hbm.at[0], wgu0_ref, sem0.at[0]),
                 pltpu.make_async_copy(wo_hbm.at[0], wo0_ref, sem0.at[1]))
        for cp in first:
            cp.start()
        for cp in first:
            cp.wait()

    def hidden(wgu):
        gu = _dot(xb_ref[...], wgu)
        return (_silu(gu[:, :tf]) * gu[:, tf:]).astype(BF16)

    def finish(h, wo, acc_ref):
        rows = tm // ROW_SPLIT
        for r in range(ROW_SPLIT):
            rs = slice(r * rows, (r + 1) * rows)
            y = _dot(h[rs], wo)
            if acc_ref is not None:
                y = y + acc_ref[rs, :]
            y = _layer_norm(alpha * x_ref[rs, :] + 0.5 * y, g_ref[...], b_ref[...])
            o_ref[rs, :] = y
            ob_ref[rs, :] = y.astype(BF16)

    if nf == 1:
        finish(hidden(wgu0_ref[...]), wo0_ref[...], None)
        return

    for cp in tile_copies(1, 1):
        cp.start()
    acc_ref[...] = _dot(hidden(wgu0_ref[...]), wo0_ref[...])

    def body(j, carry):
        slot = lax.rem(j, 2)
        for cp in tile_copies(j + 1, 1 - slot):
            cp.start()
        for cp in tile_copies(j, slot):
            cp.wait()
        acc_ref[...] += _dot(hidden(wgu_buf[slot]), wo_buf[slot])
        return carry

    lax.fori_loop(1, nf - 1, body, 0)
    slot = (nf - 1) % 2
    for cp in tile_copies(nf - 1, slot):
        cp.wait()
    finish(hidden(wgu_buf[slot]), wo_buf[slot], acc_ref)


def ffn_ln(x, xb, w_gu, w_out, g, b, *, alpha, tm=512):
    t, d = x.shape
    nf, _, tf2 = w_gu.shape
    tf = tf2 // 2
    tm = min(tm, t)
    return pl.pallas_call(
        functools.partial(_ffn_ln_kernel, alpha=alpha),
        grid=(t // tm,),
        in_specs=[
            pl.BlockSpec((tm, d), lambda i: (i, 0)),
            pl.BlockSpec((tm, d), lambda i: (i, 0)),
            pl.BlockSpec(memory_space=pl.ANY),
            pl.BlockSpec(memory_space=pl.ANY),
            pl.BlockSpec((1, d), lambda i: (0, 0)),
            pl.BlockSpec((1, d), lambda i: (0, 0)),
        ],
        out_specs=[pl.BlockSpec((tm, d), lambda i: (i, 0)), pl.BlockSpec((tm, d), lambda i: (i, 0))],
        out_shape=[jax.ShapeDtypeStruct((t, d), F32), jax.ShapeDtypeStruct((t, d), BF16)],
        scratch_shapes=[pltpu.VMEM((tm, d), F32),
                        pltpu.VMEM((d, tf2), BF16), pltpu.VMEM((tf, d), BF16),
                        pltpu.VMEM((2, d, tf2), BF16), pltpu.VMEM((2, tf, d), BF16),
                        pltpu.SemaphoreType.DMA((2,)), pltpu.SemaphoreType.DMA((2, 2))],
        compiler_params=_cparams(("arbitrary",)),
        name="ffn_ln",
    )(x, xb, w_gu, w_out.reshape(nf, tf, d), g.reshape(1, d), b.reshape(1, d))


def _tile_ffn_w_in(w_in, tf):
    d, f2 = w_in.shape
    nf = f2 // 2 // tf
    return w_in.astype(BF16).reshape(d, 2, nf, tf).transpose(2, 0, 1, 3).reshape(nf, d, 2 * tf)


def _matmul_kernel(x_ref, w_ref, o_ref):
    o_ref[...] = _dot(x_ref[...], w_ref[...]).astype(o_ref.dtype)


def matmul(xb, w, *, out_dtype=F32, tm=512, tn=None):
    t, k = xb.shape
    n = w.shape[1]
    tm = min(tm, t)
    tn = n if tn is None else min(tn, n)
    return pl.pallas_call(
        _matmul_kernel,
        grid=(n // tn, t // tm),
        in_specs=[pl.BlockSpec((tm, k), lambda j, i: (i, 0)), pl.BlockSpec((k, tn), lambda j, i: (0, j))],
        out_specs=pl.BlockSpec((tm, tn), lambda j, i: (i, j)),
        out_shape=jax.ShapeDtypeStruct((t, n), out_dtype),
        compiler_params=_cparams(("parallel", "parallel")),
        name="matmul",
    )(xb, w)


def _proj_ln_kernel(*refs, alpha, n_parts):
    x_ref = refs[0]
    parts = refs[1:1 + n_parts]
    w_ref, g_ref, b_ref, o_ref, ob_ref = refs[1 + n_parts:]
    kp = parts[0].shape[1]
    rows = x_ref.shape[0] // ROW_SPLIT
    for r in range(ROW_SPLIT):
        rs = slice(r * rows, (r + 1) * rows)
        acc = alpha * x_ref[rs, :]
        for p in range(n_parts):
            acc = acc + _dot(parts[p][rs, :], w_ref[p * kp:(p + 1) * kp, :])
        y = _layer_norm(acc, g_ref[...], b_ref[...])
        o_ref[rs, :] = y
        ob_ref[rs, :] = y.astype(BF16)


def proj_ln(x, parts, w, g, b, *, alpha, tm=512):
    t, d = x.shape
    tm = min(tm, t)
    n_parts = len(parts)
    kp = parts[0].shape[1]
    return pl.pallas_call(
        functools.partial(_proj_ln_kernel, alpha=alpha, n_parts=n_parts),
        grid=(t // tm,),
        in_specs=[pl.BlockSpec((tm, d), lambda i: (i, 0))]
        + [pl.BlockSpec((tm, kp), lambda i: (i, 0)) for _ in parts]
        + [pl.BlockSpec((n_parts * kp, d), lambda i: (0, 0)),
           pl.BlockSpec((1, d), lambda i: (0, 0)), pl.BlockSpec((1, d), lambda i: (0, 0))],
        out_specs=[pl.BlockSpec((tm, d), lambda i: (i, 0)), pl.BlockSpec((tm, d), lambda i: (i, 0))],
        out_shape=[jax.ShapeDtypeStruct((t, d), F32), jax.ShapeDtypeStruct((t, d), BF16)],
        compiler_params=_cparams(("parallel",)),
        name="proj_ln",
    )(x, *parts, w, g.reshape(1, d), b.reshape(1, d))


_LEVELS = tuple(CHUNK >> (s + 1) for s in range(int(math.log2(CHUNK)) - 1))
_UNIT_LEVEL = len(_LEVELS)
_DIAG_LEVEL = _UNIT_LEVEL + 1
GW = HEADS * LANES


def _gla_tables():
    c = CHUNK
    i = np.arange(c)[:, None]
    t = np.arange(c)[None, :]
    blocks = [t <= i, t > i]
    for h in _LEVELS:
        lower = (i // h) % 2 == 1
        blocks.append(np.where(lower, (t >= (i // h) * h) & (t <= i), (t > i) & (t <= (i // h + 1) * h - 1)))
    sums = np.concatenate(blocks, axis=0).astype(np.float32)
    sums = np.concatenate([sums] * 3, axis=1)
    msb = np.floor(np.log2(np.maximum(i ^ t, 1))).astype(np.int32)
    lvl = np.where(t < i, _UNIT_LEVEL - msb, np.where(t == i, _DIAG_LEVEL, -1)).astype(np.int32)
    return sums, lvl


def _gla_chunk(qs, ks, vs, gs, sums_ref, lvl, st_ref):
    c, dk = qs[0].shape
    e_all = _dot(sums_ref[...], _split3(jnp.concatenate(gs, axis=-1)))
    outs = []
    for h, (q, k, v, g) in enumerate(zip(qs, ks, vs, gs)):
        e = e_all[:, dk * h:dk * (h + 1)]
        st = st_ref[h]
        o = _dot_nt((q * jnp.exp(e[0:c])).astype(BF16), st.astype(BF16))
        attn = jnp.where(lvl == _DIAG_LEVEL, jnp.sum(q * k, axis=-1, keepdims=True), 0.0)
        attn = jnp.where(lvl == _UNIT_LEVEL, _dot_nt((q * jnp.exp(g)).astype(BF16), k.astype(BF16)), attn)
        for l in range(len(_LEVELS)):
            x = jnp.exp(e[(2 + l) * c:(3 + l) * c])
            attn = jnp.where(lvl == l, _dot_nt((q * x).astype(BF16), (k * x).astype(BF16)), attn)
        vb = v.astype(BF16)
        outs.append(o + _dot(attn.astype(BF16), vb))
        ks_ = (k * jnp.exp(e[c:2 * c])).astype(BF16)
        st_ref[h] = st * jnp.exp(e[c - 1:c]) + _dot_tn(vb, ks_)
    return outs


def _heads(x):
    return [x[:, h * LANES:(h + 1) * LANES] for h in range(HEADS)]


def _store_heads(o_ref, sl, outs):
    for h, o in enumerate(outs):
        o_ref[sl, h * LANES:(h + 1) * LANES] = o.astype(o_ref.dtype)


def _gla_kernel(q_ref, k_ref, v_ref, r_ref, lr_ref, gw_ref, gb_ref, ng_ref, sums_ref, lvl_ref, o_ref, st_ref):
    @pl.when(pl.program_id(1) == 0)
    def _():
        st_ref[...] = jnp.zeros_like(st_ref)

    lvl = lvl_ref[...]
    ng = ng_ref[...]

    def body(ci, carry):
        sl = pl.ds(pl.multiple_of(ci * CHUNK, CHUNK), CHUNK)
        g = _log_sigmoid(_dot(lr_ref[sl, :].astype(BF16), gw_ref[...]) + gb_ref[...]) / GLA_GATE_NORM
        outs = _gla_chunk(_heads(q_ref[sl, :] * GLA_DK ** -0.5), _heads(k_ref[sl, :]), _heads(v_ref[sl, :]),
                          _heads(g), sums_ref, lvl, st_ref)
        r = _heads(_silu(r_ref[sl, :]))
        _store_heads(o_ref, sl, [_rms_norm(o, ng) * rh for o, rh in zip(outs, r)])
        return carry

    lax.fori_loop(0, q_ref.shape[0] // CHUNK, body, 0, unroll=CHUNK_UNROLL)


def _hgrn_kernel(q_ref, f_ref, v_ref, r_ref, lb_ref, ng_ref, sums_ref, lvl_ref, o_ref, st_ref):
    @pl.when(pl.program_id(1) == 0)
    def _():
        st_ref[...] = jnp.zeros_like(st_ref)

    lvl = lvl_ref[...]
    ng = ng_ref[...]
    lb = lb_ref[...]
    lb_floor = jnp.maximum(lb, LB_EPS)

    def body(ci, carry):
        sl = pl.ds(pl.multiple_of(ci * CHUNK, CHUNK), CHUNK)
        f_pre = f_ref[sl, :]
        t = jnp.exp(-jnp.abs(f_pre))
        r = 1.0 / (1.0 + t)
        sig_pos = jnp.where(f_pre >= 0, r, t * r)
        sig_neg = jnp.where(f_pre >= 0, t * r, r)
        g = jnp.log(lb_floor + (1.0 - lb) * sig_pos)
        k = (1.0 - lb) * sig_neg
        q = _silu(q_ref[sl, :]) * HGRN_DK ** -0.5
        outs = _gla_chunk(_heads(q), _heads(k), _heads(v_ref[sl, :]), _heads(g), sums_ref, lvl, st_ref)
        r = _heads(_silu(r_ref[sl, :]))
        _store_heads(o_ref, sl, [_rms_norm(o, ng) * rh for o, rh in zip(outs, r)])
        return carry

    lax.fori_loop(0, q_ref.shape[0] // CHUNK, body, 0, unroll=CHUNK_UNROLL)


def _group_block(tt, nt, idx):
    return pl.BlockSpec((tt, GW), lambda b, t: (b * nt + t, idx))


def _const_block(shape):
    return pl.BlockSpec(shape, lambda *_: tuple(0 for _ in shape))


def gla_mixer(proj, gate_w, gate_b, norm_g, *, batch, tt=512):
    t = proj.shape[0]
    s = t // batch
    tt = min(tt, s)
    nt = s // tt
    sums, lvl = _gla_tables()
    gb = functools.partial(_group_block, tt, nt)
    return pl.pallas_call(
        _gla_kernel,
        grid=(batch, nt),
        in_specs=[gb(0), gb(1), gb(2), gb(3),
                  pl.BlockSpec((tt, LANES), lambda b, t: (b * nt + t, 4 * HEADS)),
                  _const_block((LANES, GW)), _const_block((1, GW)),
                  _const_block((1, LANES)), _const_block(sums.shape), _const_block(lvl.shape)],
        out_specs=gb(0),
        out_shape=jax.ShapeDtypeStruct((t, GW), BF16),
        scratch_shapes=[pltpu.VMEM((HEADS, LANES, LANES), F32)],
        compiler_params=_cparams(("parallel", "arbitrary")),
        name="gla",
    )(proj, proj, proj, proj, proj, gate_w, gate_b, norm_g.reshape(1, LANES),
      jnp.asarray(sums, BF16), jnp.asarray(lvl))


def hgrn_mixer(proj, lower_bound, norm_g, *, batch, tt=512):
    t = proj.shape[0]
    s = t // batch
    tt = min(tt, s)
    nt = s // tt
    sums, lvl = _gla_tables()
    gb = functools.partial(_group_block, tt, nt)
    return pl.pallas_call(
        _hgrn_kernel,
        grid=(batch, nt),
        in_specs=[gb(0), gb(1), gb(2), gb(3),
                  _const_block((1, GW)),
                  _const_block((1, LANES)), _const_block(sums.shape), _const_block(lvl.shape)],
        out_specs=gb(0),
        out_shape=jax.ShapeDtypeStruct((t, GW), BF16),
        scratch_shapes=[pltpu.VMEM((HEADS, LANES, LANES), F32)],
        compiler_params=_cparams(("parallel", "arbitrary")),
        name="hgrn",
    )(proj, proj, proj, proj, lower_bound.reshape(1, GW), norm_g.reshape(1, LANES),
      jnp.asarray(sums, BF16), jnp.asarray(lvl))


def _mlstm_select_table():
    sel = np.zeros((HEADS, 3 * LANES, 2 * LANES), np.float32)
    for h in range(HEADS):
        for p in range(3):
            sel[h, p * LANES + HEADS + h, :LANES] = 1.0
            sel[h, p * LANES + h, LANES:] = 1.0
            sel[h, p * LANES + HEADS + h, LANES:] = -1.0
    return sel


def _mlstm_kernel(q_ref, k_ref, v_ref, og_ref, if_ref, cw_q_ref, cw_k_ref, gb_ref, sel_ref, o_ref,
                  st_ref, m_ref, qx_ref, kx_ref):
    tt = q_ref.shape[0]

    @pl.when(pl.program_id(1) == 0)
    def _():
        st_ref[...] = jnp.zeros_like(st_ref)
        m_ref[...] = jnp.zeros_like(m_ref)
        qx_ref[0:CONV_PAD, :] = jnp.zeros((CONV_PAD, GW), F32)
        kx_ref[0:CONV_PAD, :] = jnp.zeros((CONV_PAD, GW), F32)

    qx_ref[CONV_PAD:CONV_PAD + tt, :] = q_ref[...]
    kx_ref[CONV_PAD:CONV_PAD + tt, :] = k_ref[...]

    c = CHUNK

    def conv_silu(ext, cw, r0):
        xw = ext[pl.ds(r0, c + CONV_PAD), :]
        acc = xw[CONV_PAD:] * cw[CONV_WIDTH - 1:CONV_WIDTH, :]
        for back in range(1, CONV_WIDTH):
            acc = acc + pltpu.roll(xw, back, axis=0)[CONV_PAD:] * cw[CONV_WIDTH - 1 - back:CONV_WIDTH - back, :]
        return _silu(acc)

    row = lax.broadcasted_iota(jnp.int32, (c, c), 0)
    col = lax.broadcasted_iota(jnp.int32, (c, c), 1)
    causal = col <= row
    eye = col == row
    tril3 = jnp.concatenate([jnp.where(causal, 1.0, 0.0).astype(BF16)] * 3, axis=1)
    ones3 = jnp.ones((c, 3 * c), BF16)
    row_l = lax.broadcasted_iota(jnp.int32, (c, LANES), 0)
    lane_l = lax.broadcasted_iota(jnp.int32, (c, LANES), 1)
    ones_blk = jnp.ones((c, LANES), F32)
    gbias = gb_ref[...]

    def prefix_max(a):
        s = 1
        while s < c:
            a = jnp.maximum(a, jnp.where(row_l >= s, pltpu.roll(a, s, axis=0), MASK_NEG))
            s *= 2
        return a

    def body(ci, carry):
        r0 = pl.multiple_of(ci * c, c)
        sl = pl.ds(r0, c)
        gates = if_ref[sl, :] + gbias
        cum_all = _dot(tril3, _split3(_log_sigmoid(gates)))
        x3 = _split3(jnp.where(lane_l < HEADS, gates, cum_all), axis=1)
        qc = _heads(conv_silu(qx_ref, cw_q_ref, r0))
        kc = _heads(conv_silu(kx_ref, cw_k_ref, r0) * MLSTM_DK ** -0.5)
        vs = _heads(v_ref[sl, :])
        og = _heads(_sigmoid(og_ref[sl, :]))
        outs = []
        for h in range(HEADS):
            ca = _dot(x3, sel_ref[h])
            cum = ca[:, :LANES]
            a = ca[:, LANES:]
            m_st = m_ref[h]
            m_t = cum + jnp.maximum(m_st, prefix_max(a))
            w_inter = jnp.exp(cum + m_st - m_t)
            a_rows = _dot(ones3, _split3(jnp.where(eye, a[:, :c], 0.0)))
            w_intra = jnp.where(causal, jnp.exp(cum[:, :c] + a_rows - m_t[:, :c]), 0.0)
            qb = qc[h].astype(BF16)
            kf = kc[h]
            v_ext = jnp.concatenate([vs[h], ones_blk], axis=-1).astype(BF16)
            scores = _dot_nt(qb, kf.astype(BF16)) * w_intra
            st = st_ref[h]
            nd = (jnp.concatenate([w_inter, w_inter], axis=-1) * _dot(qb, st.astype(BF16))
                  + _dot(scores.astype(BF16), v_ext))
            den = jnp.maximum(jnp.abs(nd[:, LANES:]), jnp.exp(-m_t))
            outs.append(nd[:, :LANES] / den * og[h])
            cum_last = cum[c - 1:c, :]
            lli = cum_last + a
            log_last_inter = cum_last + m_st
            m_new = jnp.maximum(log_last_inter, jnp.max(lli, axis=0, keepdims=True))
            wk = jnp.exp(lli - m_new)
            dec = jnp.exp(log_last_inter - m_new)
            st_ref[h] = jnp.concatenate([dec, dec], axis=-1) * st + _dot_tn((kf * wk).astype(BF16), v_ext)
            m_ref[h] = m_new
        _store_heads(o_ref, sl, outs)
        return carry

    lax.fori_loop(0, tt // c, body, 0, unroll=CHUNK_UNROLL)
    qx_ref[0:CONV_PAD, :] = qx_ref[tt:tt + CONV_PAD, :]
    kx_ref[0:CONV_PAD, :] = kx_ref[tt:tt + CONV_PAD, :]


def mlstm_mixer(proj, conv_w_q, conv_w_k, gate_b, *, batch, tt=512):
    t = proj.shape[0]
    s = t // batch
    tt = min(tt, s)
    nt = s // tt
    gb = functools.partial(_group_block, tt, nt)
    return pl.pallas_call(
        _mlstm_kernel,
        grid=(batch, nt),
        in_specs=[gb(0), gb(1), gb(2), gb(3),
                  pl.BlockSpec((tt, LANES), lambda b, t: (b * nt + t, 4 * HEADS)),
                  _const_block((CONV_WIDTH, GW)), _const_block((CONV_WIDTH, GW)),
                  _const_block((1, LANES)), _const_block((HEADS, 3 * LANES, 2 * LANES))],
        out_specs=gb(0),
        out_shape=jax.ShapeDtypeStruct((t, GW), BF16),
        scratch_shapes=[pltpu.VMEM((HEADS, LANES, 2 * LANES), F32), pltpu.VMEM((HEADS, 1, LANES), F32),
                        pltpu.VMEM((CONV_PAD + tt, GW), F32), pltpu.VMEM((CONV_PAD + tt, GW), F32)],
        compiler_params=_cparams(("parallel", "arbitrary")),
        name="mlstm",
    )(proj, proj, proj, proj, proj, conv_w_q, conv_w_k, gate_b, jnp.asarray(_mlstm_select_table(), BF16))


LOG2E = math.log2(math.e)


def _diff_attn_kernel(sc_ref, q_ref, k_ref, vt_ref, bias_ref, ng_ref, o_ref,
                      s_ref, p_ref, mx_ref, m_ref, c_ref, a_ref):
    qi = pl.program_id(2)
    q = q_ref[...].astype(F32) * (DIFF_DK ** -0.5 * LOG2E)
    lane = lax.broadcasted_iota(jnp.int32, q.shape, 1)
    qs = (jnp.where(lane < DIFF_DK, q, 0.0).astype(BF16), jnp.where(lane >= DIFF_DK, q, 0.0).astype(BF16))
    m_ref[...] = jnp.full(m_ref.shape, MASK_NEG, F32)
    c_ref[...] = jnp.ones_like(c_ref)
    a_ref[...] = jnp.zeros_like(a_ref)
    p_ref[1] = jnp.zeros(p_ref.shape[1:], BF16)
    tk = vt_ref.shape[2]
    ones_rows = jnp.where(lax.broadcasted_iota(jnp.int32, (BF16_SUBLANES, tk), 0) == 0, 1.0, 0.0).astype(BF16)

    def scores(kt, slot):
        kk = k_ref[pl.ds(pl.multiple_of(kt * tk, tk), tk), :]
        tile = jnp.minimum(qi - kt, 2)
        for mp in range(2):
            s = _dot_nt(kk, qs[mp]) + bias_ref[tile]
            s_ref[slot, mp] = s
            mx_ref[slot, mp] = jnp.max(s, axis=0, keepdims=True)

    def values(kt, slot):
        vt = jnp.concatenate([vt_ref[kt], ones_rows], axis=0)
        for mp in range(2):
            a_ref[mp] = a_ref[mp] * c_ref[mp] + _dot(vt, p_ref[slot, mp])

    scores(0, 0)

    def step(ki, cur):
        nxt = 1 - cur
        scores(jnp.minimum(ki + 1, qi), nxt)
        values(jnp.maximum(ki - 1, 0), nxt)
        for mp in range(2):
            m_old = m_ref[mp]
            m_new = jnp.maximum(m_old, mx_ref[cur, mp])
            p_ref[cur, mp] = jnp.exp2(s_ref[cur, mp] - m_new).astype(BF16)
            c_ref[mp] = jnp.exp2(m_old - m_new)
            m_ref[mp] = m_new

    def pair(j, carry):
        step(2 * j, 0)
        step(2 * j + 1, 1)
        return carry

    n_tiles = qi + 1
    lax.fori_loop(0, n_tiles // 2, pair, 0)

    @pl.when(lax.rem(n_tiles, 2) == 1)
    def _():
        step(qi, 0)

    values(qi, lax.rem(qi, 2))
    lam = sc_ref[0]
    dv = o_ref.shape[1]
    out_t = (a_ref[0, 0:dv] / a_ref[0, dv:dv + 1] - lam * (a_ref[1, 0:dv] / a_ref[1, dv:dv + 1]))
    out_t = out_t * lax.rsqrt(jnp.mean(out_t * out_t, axis=0, keepdims=True) + LN_EPS)
    o_ref[...] = (out_t.T * ng_ref[...] * sc_ref[1]).astype(o_ref.dtype)


def diff_attn_mixer(qk, vt, bias_tiles, scalars, norm_g, *, batch, tq):
    t = qk.shape[0]
    s = t // batch
    nt = s // tq
    dve = LANES + BF16_SUBLANES
    return pl.pallas_call(
        _diff_attn_kernel,
        grid=(batch, HEADS, nt),
        in_specs=[pl.BlockSpec(memory_space=pltpu.SMEM),
                  pl.BlockSpec((tq, LANES), lambda b, h, t: (b * nt + t, h)),
                  pl.BlockSpec((s, LANES), lambda b, h, t: (b, HEADS + h)),
                  pl.BlockSpec((None, nt, LANES, tq), lambda b, h, t: (b * HEADS + h, 0, 0, 0)),
                  pl.BlockSpec((None, 3, tq, tq), lambda b, h, t: (h, 0, 0, 0)),
                  _const_block((1, LANES))],
        out_specs=pl.BlockSpec((tq, LANES), lambda b, h, t: (b * nt + t, h)),
        out_shape=jax.ShapeDtypeStruct((t, HEADS * LANES), BF16),
        scratch_shapes=[pltpu.VMEM((2, 2, tq, tq), F32), pltpu.VMEM((2, 2, tq, tq), BF16),
                        pltpu.VMEM((2, 2, 1, tq), F32), pltpu.VMEM((2, 1, tq), F32), pltpu.VMEM((2, 1, tq), F32),
                        pltpu.VMEM((2, dve, tq), F32)],
        compiler_params=_cparams(("parallel", "parallel", "arbitrary")),
        name="diff_attn",
    )(scalars, qk, qk, vt, bias_tiles, norm_g.reshape(1, LANES))


def _t5_bucket_map(rel):
    n = np.maximum(rel, 0)
    max_exact = T5_BUCKETS // 2
    large = max_exact + (np.log(np.maximum(n, 1).astype(np.float32) / np.float32(max_exact))
                         / np.float32(math.log(T5_MAX_DIST / max_exact)) * (T5_BUCKETS - max_exact)).astype(np.int32)
    large = np.clip(large, max_exact, T5_BUCKETS - 1)
    return np.where(n < max_exact, n, large).astype(np.int32)


def t5_bias_tiles(t5_table, tq):
    assert tq >= T5_MAX_DIST
    j = np.arange(tq)[:, None]
    i = np.arange(tq)[None, :]
    rel = np.stack([d * tq + i - j for d in range(3)])
    bucket = jnp.asarray(_t5_bucket_map(rel).astype(np.int8))
    table = t5_table.astype(F32)
    tiles = jnp.zeros((HEADS,) + rel.shape, F32)
    for b in range(T5_BUCKETS):
        tiles = jnp.where(bucket[None] == b, table[b][:, None, None, None], tiles)
    return jnp.where(jnp.asarray(rel >= 0)[None], tiles * LOG2E, MASK_NEG)


def _xattn_kernel(xb_ref, wq_ref, k_ref, v_ref, o_ref):
    d = xb_ref.shape[1]
    hd = d // N_XHEADS
    q = _dot(xb_ref[...], wq_ref[...]).astype(BF16)
    for h in range(N_XHEADS):
        hs = slice(h * hd, (h + 1) * hd)
        s = _dot_nt(q[:, hs], k_ref[:, hs]) * hd ** -0.5
        s = s - jnp.max(s, axis=-1, keepdims=True)
        p = jnp.exp(s)
        p = p / jnp.sum(p, axis=-1, keepdims=True)
        o_ref[:, hs] = _dot(p.astype(BF16), v_ref[:, hs]).astype(o_ref.dtype)


def xattn(xb, wq, kv, *, batch, tm=512):
    t, d = xb.shape
    s = t // batch
    tm = min(tm, s)
    nt = s // tm
    n_mem = kv.shape[0] // batch
    return pl.pallas_call(
        _xattn_kernel,
        grid=(t // tm,),
        in_specs=[pl.BlockSpec((tm, d), lambda i: (i, 0)),
                  pl.BlockSpec((d, d), lambda i: (0, 0)),
                  pl.BlockSpec((n_mem, d), lambda i: (i // nt, 0)),
                  pl.BlockSpec((n_mem, d), lambda i: (i // nt, 1))],
        out_specs=pl.BlockSpec((tm, d), lambda i: (i, 0)),
        out_shape=jax.ShapeDtypeStruct((t, d), BF16),
        compiler_params=_cparams(("parallel",)),
        name="xattn",
    )(xb, wq, kv, kv)


def _pad_heads(w, dk):
    lead = w.shape[:-1]
    w = w.reshape(*lead, HEADS, dk)
    w = jnp.pad(w, [(0, 0)] * len(lead) + [(0, 0), (0, LANES - dk)])
    return w.reshape(*lead, HEADS * LANES)


def _pad_cols(w, n):
    return jnp.pad(w, [(0, 0)] * (w.ndim - 1) + [(0, n - w.shape[-1])])


def _split_in_proj(w_in):
    gw = HEADS * LANES
    sizes = (HEADS * GLA_DK, HEADS * GLA_DK, gw, GLA_GATE_RANK, gw,
             HEADS * HGRN_DK, HEADS * HGRN_DK, gw, gw,
             2 * HEADS * DIFF_DK, 2 * HEADS * DIFF_DK, gw,
             HEADS * MLSTM_DK, HEADS * MLSTM_DK, gw, 2 * HEADS, gw)
    assert sum(sizes) == w_in.shape[-1]
    (a_q, a_k, a_v, a_lr, a_r, b_q, b_f, b_i, b_g, c_q, c_k, c_v,
     d_q, d_k, d_v, d_if, d_o) = jnp.split(w_in, list(np.cumsum(sizes)[:-1]), axis=-1)
    w_gla = jnp.concatenate([_pad_heads(a_q, GLA_DK), _pad_heads(a_k, GLA_DK), a_v, a_r, _pad_cols(a_lr, LANES)], -1)
    w_hgrn = jnp.concatenate([b_q, b_f, b_i, b_g], -1)
    w_diff = jnp.concatenate([c_q, c_k, c_v], -1)
    w_mlstm = jnp.concatenate([_pad_heads(d_q, MLSTM_DK), _pad_heads(d_k, MLSTM_DK), d_v, d_o, _pad_cols(d_if, LANES)], -1)
    return [w.astype(BF16) for w in (w_gla, w_hgrn, w_diff, w_mlstm)]


def kernel(x, mem, ln_g, ln_b, ffn_w_in, ffn_w_out, w_in, w_out, gla_gate_w, gla_gate_b, gla_norm_g, hgrn_lb, hgrn_norm_g, diff_lambda, diff_norm_g, t5_table, mlstm_conv_w, mlstm_gate_b, xattn_w_q, xattn_w_kv, xattn_w_o):
    batch, seq, d = x.shape
    depth = ln_g.shape[0]
    alpha = (2 * depth) ** 0.25
    t = batch * seq

    sm = jax.nn.softmax(hgrn_lb.astype(F32), axis=0)
    lower_bounds = jnp.clip(jnp.cumsum(sm, axis=0) - sm[0], 0.0, 1.0 - 1e-6)
    tq = min(512, seq)
    bias_tiles = t5_bias_tiles(t5_table, tq)
    tf = min(FFN_TILE, ffn_w_out.shape[2])

    x = x.reshape(t, d)
    xb = x.astype(BF16)
    memb = mem.reshape(-1, d).astype(BF16)
    for l in range(depth):
        x, xb = ffn_ln(x, xb, _tile_ffn_w_in(ffn_w_in[l, 0], tf), ffn_w_out[l, 0].astype(BF16), ln_g[l, 0], ln_b[l, 0], alpha=alpha)

        w_gla, w_hgrn, w_diff, w_mlstm = _split_in_proj(w_in[l])
        o_a = gla_mixer(matmul(xb, w_gla),
                        _pad_heads(_pad_cols(gla_gate_w[l].T, LANES).T, GLA_DK).astype(BF16),
                        _pad_heads(gla_gate_b[l].reshape(1, -1), GLA_DK), gla_norm_g[l], batch=batch)
        o_b = hgrn_mixer(matmul(xb, w_hgrn), lower_bounds[l], hgrn_norm_g[l], batch=batch)
        lambda_init = 0.8 - 0.6 * math.exp(-0.3 * l)
        lq1, lk1, lq2, lk2 = [diff_lambda[l, j].astype(F32) for j in range(4)]
        lam = jnp.exp(jnp.sum(lq1 * lk1)) - jnp.exp(jnp.sum(lq2 * lk2)) + lambda_init
        p_c = matmul(xb, w_diff, out_dtype=BF16)
        vt_c = (p_c[:, 2 * GW:].reshape(batch, seq // tq, tq, HEADS, LANES).transpose(0, 3, 1, 4, 2)
                .reshape(batch * HEADS, seq // tq, LANES, tq))
        o_c = diff_attn_mixer(p_c, vt_c, bias_tiles,
                              jnp.stack([lam, jnp.asarray(1.0 - lambda_init, F32)]).astype(F32), diff_norm_g[l], batch=batch, tq=tq)
        cw = mlstm_conv_w[l]
        o_d = mlstm_mixer(matmul(xb, w_mlstm), _pad_heads(cw[:, :HEADS * MLSTM_DK], MLSTM_DK),
                          _pad_heads(cw[:, HEADS * MLSTM_DK:], MLSTM_DK),
                          _pad_cols(mlstm_gate_b[l].reshape(1, -1), LANES), batch=batch)
        x, xb = proj_ln(x, [o_a, o_b, o_c, o_d], w_out[l].astype(BF16), ln_g[l, 1], ln_b[l, 1], alpha=alpha)

        kv = matmul(memb, xattn_w_kv[l].astype(BF16), out_dtype=BF16, tn=1024)
        att = xattn(xb, xattn_w_q[l].astype(BF16), kv, batch=batch)
        x, xb = proj_ln(x, [att], xattn_w_o[l].astype(BF16), ln_g[l, 2], ln_b[l, 2], alpha=alpha)

        x, xb = ffn_ln(x, xb, _tile_ffn_w_in(ffn_w_in[l, 1], tf), ffn_w_out[l, 1].astype(BF16), ln_g[l, 3], ln_b[l, 3], alpha=alpha)
    return x.reshape(batch, seq, d)
```

```python
import functools
import math

import numpy as np
import jax
import jax.numpy as jnp
from jax import lax
from jax.experimental import pallas as pl
from jax.experimental.pallas import tpu as pltpu

F32 = jnp.float32
BF16 = jnp.bfloat16

HEADS = 4
N_XHEADS = 4
LANES = 128
BF16_SUBLANES = 16
GLA_DK = 64
HGRN_DK = 128
DIFF_DK = 64
MLSTM_DK = 64
GLA_GATE_RANK = 16
GLA_GATE_NORM = 16.0
CONV_WIDTH = 4
CHUNK = 64
T5_BUCKETS = 32
T5_MAX_DIST = 128
LN_EPS = 1e-5
LB_EPS = 1e-12
MASK_NEG = -1e30
CONV_PAD = 8
CHUNK_UNROLL = 4
PAIR = 2

VMEM_LIMIT = 56 * 1024 * 1024


def _cparams(sem):
    return pltpu.CompilerParams(dimension_semantics=sem, vmem_limit_bytes=VMEM_LIMIT)


def _sigmoid(x):
    return 1.0 / (1.0 + jnp.exp(-x))


def _silu(x):
    return x * _sigmoid(x)


def _log_sigmoid(x):
    return jnp.minimum(x, 0.0) - jnp.log1p(jnp.exp(-jnp.abs(x)))


def _layer_norm(y, g, b):
    mu = jnp.mean(y, axis=-1, keepdims=True)
    d = y - mu
    var = jnp.mean(d * d, axis=-1, keepdims=True)
    return d * lax.rsqrt(var + LN_EPS) * g + b


def _rms_norm(y, g):
    return y * lax.rsqrt(jnp.mean(y * y, axis=-1, keepdims=True) + LN_EPS) * g


def _dot(a, b):
    return jnp.dot(a, b, preferred_element_type=F32)


def _dot_nt(a, b):
    return lax.dot_general(a, b, (((1,), (1,)), ((), ())), preferred_element_type=F32)


def _dot_tn(a, b):
    return lax.dot_general(a, b, (((0,), (0,)), ((), ())), preferred_element_type=F32)


def _split3(g):
    hi = g.astype(BF16)
    r1 = g - hi.astype(F32)
    mid = r1.astype(BF16)
    lo = (r1 - mid.astype(F32)).astype(BF16)
    return jnp.concatenate([hi, mid, lo], axis=0)


def _ffn_ln_kernel(x_ref, xb_ref, wg_ref, wu_ref, wo_ref, g_ref, b_ref, o_ref, ob_ref, acc_ref, *, alpha):
    j = pl.program_id(1)

    @pl.when(j == 0)
    def _():
        acc_ref[...] = jnp.zeros_like(acc_ref)

    xb = xb_ref[...]
    gate = _dot(xb, wg_ref[...])
    up = _dot(xb, wu_ref[...])
    h = (_silu(gate) * up).astype(BF16)
    acc_ref[...] += _dot(h, wo_ref[...])

    @pl.when(j == pl.num_programs(1) - 1)
    def _():
        y = _layer_norm(alpha * x_ref[...] + 0.5 * acc_ref[...], g_ref[...], b_ref[...])
        o_ref[...] = y
        ob_ref[...] = y.astype(BF16)


def ffn_ln(x, xb, w_in, w_out, g, b, *, alpha, tm=512, tf=512):
    t, d = x.shape
    f = w_out.shape[0]
    tm, tf = min(tm, t), min(tf, f)
    nf = f // tf
    return pl.pallas_call(
        functools.partial(_ffn_ln_kernel, alpha=alpha),
        grid=(t // tm, nf),
        in_specs=[
            pl.BlockSpec((tm, d), lambda i, j: (i, 0)),
            pl.BlockSpec((tm, d), lambda i, j: (i, 0)),
            pl.BlockSpec((d, tf), lambda i, j: (0, j)),
            pl.BlockSpec((d, tf), lambda i, j: (0, j + nf)),
            pl.BlockSpec((tf, d), lambda i, j: (j, 0)),
            pl.BlockSpec((1, d), lambda i, j: (0, 0)),
            pl.BlockSpec((1, d), lambda i, j: (0, 0)),
        ],
        out_specs=[pl.BlockSpec((tm, d), lambda i, j: (i, 0)), pl.BlockSpec((tm, d), lambda i, j: (i, 0))],
        out_shape=[jax.ShapeDtypeStruct((t, d), F32), jax.ShapeDtypeStruct((t, d), BF16)],
        scratch_shapes=[pltpu.VMEM((tm, d), F32)],
        compiler_params=_cparams(("parallel", "arbitrary")),
        name="ffn_ln",
    )(x, xb, w_in, w_in, w_out, g.reshape(1, d), b.reshape(1, d))


def _matmul_kernel(x_ref, w_ref, o_ref):
    o_ref[...] = _dot(x_ref[...], w_ref[...]).astype(o_ref.dtype)


def matmul(xb, w, *, out_dtype=F32, tm=512, tn=None):
    t, k = xb.shape
    n = w.shape[1]
    tm = min(tm, t)
    tn = n if tn is None else min(tn, n)
    return pl.pallas_call(
        _matmul_kernel,
        grid=(n // tn, t // tm),
        in_specs=[pl.BlockSpec((tm, k), lambda j, i: (i, 0)), pl.BlockSpec((k, tn), lambda j, i: (0, j))],
        out_specs=pl.BlockSpec((tm, tn), lambda j, i: (i, j)),
        out_shape=jax.ShapeDtypeStruct((t, n), out_dtype),
        compiler_params=_cparams(("parallel", "parallel")),
        name="matmul",
    )(xb, w)


def _proj_ln_kernel(*refs, alpha, n_parts):
    x_ref = refs[0]
    parts = refs[1:1 + n_parts]
    w_ref, g_ref, b_ref, o_ref, ob_ref = refs[1 + n_parts:]
    kp = parts[0].shape[1]
    acc = alpha * x_ref[...]
    for p in range(n_parts):
        acc = acc + _dot(parts[p][...], w_ref[p * kp:(p + 1) * kp, :])
    y = _layer_norm(acc, g_ref[...], b_ref[...])
    o_ref[...] = y
    ob_ref[...] = y.astype(BF16)


def proj_ln(x, parts, w, g, b, *, alpha, tm=512):
    t, d = x.shape
    tm = min(tm, t)
    n_parts = len(parts)
    kp = parts[0].shape[1]
    return pl.pallas_call(
        functools.partial(_proj_ln_kernel, alpha=alpha, n_parts=n_parts),
        grid=(t // tm,),
        in_specs=[pl.BlockSpec((tm, d), lambda i: (i, 0))]
        + [pl.BlockSpec((tm, kp), lambda i: (i, 0)) for _ in parts]
        + [pl.BlockSpec((n_parts * kp, d), lambda i: (0, 0)),
           pl.BlockSpec((1, d), lambda i: (0, 0)), pl.BlockSpec((1, d), lambda i: (0, 0))],
        out_specs=[pl.BlockSpec((tm, d), lambda i: (i, 0)), pl.BlockSpec((tm, d), lambda i: (i, 0))],
        out_shape=[jax.ShapeDtypeStruct((t, d), F32), jax.ShapeDtypeStruct((t, d), BF16)],
        compiler_params=_cparams(("parallel",)),
        name="proj_ln",
    )(x, *parts, w, g.reshape(1, d), b.reshape(1, d))


_LEVELS = tuple(CHUNK >> (s + 1) for s in range(int(math.log2(CHUNK)) - 1))
_UNIT_LEVEL = len(_LEVELS)
_DIAG_LEVEL = _UNIT_LEVEL + 1
GW = HEADS * LANES


def _gla_tables():
    c = CHUNK
    i = np.arange(c)[:, None]
    t = np.arange(c)[None, :]
    blocks = [t <= i, t > i]
    for h in _LEVELS:
        lower = (i // h) % 2 == 1
        blocks.append(np.where(lower, (t >= (i // h) * h) & (t <= i), (t > i) & (t <= (i // h + 1) * h - 1)))
    sums = np.concatenate(blocks, axis=0).astype(np.float32)
    sums = np.concatenate([sums] * 3, axis=1)
    msb = np.floor(np.log2(np.maximum(i ^ t, 1))).astype(np.int32)
    lvl = np.where(t < i, _UNIT_LEVEL - msb, np.where(t == i, _DIAG_LEVEL, -1)).astype(np.int32)
    lvl_pair = np.full((PAIR * c, PAIR * c), -1, np.int32)
    for p in range(PAIR):
        lvl_pair[p * c:(p + 1) * c, p * c:(p + 1) * c] = lvl
    return sums, lvl_pair


def _gla_chunks(qs, ks, vs, gs, sums_ref, lvl, st_ref):
    c = CHUNK
    dk = qs[0].shape[1]
    g_all = jnp.concatenate(gs, axis=-1)
    g_wide = jnp.concatenate([g_all[p * c:(p + 1) * c] for p in range(PAIR)], axis=-1)
    e_all = _dot(sums_ref[...], _split3(g_wide))
    outs = []
    for h, (q, k, v, g) in enumerate(zip(qs, ks, vs, gs)):
        e = [e_all[:, (p * HEADS + h) * dk:(p * HEADS + h + 1) * dk] for p in range(PAIR)]
        block = lambda b: jnp.concatenate([ep[b * c:(b + 1) * c] for ep in e], axis=0)
        attn = jnp.where(lvl == _DIAG_LEVEL, jnp.sum(q * k, axis=-1, keepdims=True), 0.0)
        attn = jnp.where(lvl == _UNIT_LEVEL, _dot_nt((q * jnp.exp(g)).astype(BF16), k.astype(BF16)), attn)
        for l in range(len(_LEVELS)):
            x = jnp.exp(block(2 + l))
            attn = jnp.where(lvl == l, _dot_nt((q * x).astype(BF16), (k * x).astype(BF16)), attn)
        vb = v.astype(BF16)
        o_intra = _dot(attn.astype(BF16), vb)
        q_in = (q * jnp.exp(block(0))).astype(BF16)
        k_out = (k * jnp.exp(block(1))).astype(BF16)
        st = st_ref[h]
        o_inter = []
        for p in range(PAIR):
            rows = slice(p * c, (p + 1) * c)
            o_inter.append(_dot_nt(q_in[rows], st.astype(BF16)))
            st = st * jnp.exp(e[p][c - 1:c]) + _dot_tn(vb[rows], k_out[rows])
        st_ref[h] = st
        outs.append(o_intra + jnp.concatenate(o_inter, axis=0))
    return outs


def _heads(x):
    return [x[:, h * LANES:(h + 1) * LANES] for h in range(HEADS)]


def _store_heads(o_ref, sl, outs):
    for h, o in enumerate(outs):
        o_ref[sl, h * LANES:(h + 1) * LANES] = o.astype(o_ref.dtype)


def _gla_kernel(q_ref, k_ref, v_ref, r_ref, lr_ref, gw_ref, gb_ref, ng_ref, sums_ref, lvl_ref, o_ref, st_ref):
    @pl.when(pl.program_id(1) == 0)
    def _():
        st_ref[...] = jnp.zeros_like(st_ref)

    lvl = lvl_ref[...]
    ng = ng_ref[...]
    rows = PAIR * CHUNK

    def body(ci, carry):
        sl = pl.ds(pl.multiple_of(ci * rows, rows), rows)
        g = _log_sigmoid(_dot(lr_ref[sl, :].astype(BF16), gw_ref[...]) + gb_ref[...]) / GLA_GATE_NORM
        outs = _gla_chunks(_heads(q_ref[sl, :] * GLA_DK ** -0.5), _heads(k_ref[sl, :]), _heads(v_ref[sl, :]),
                           _heads(g), sums_ref, lvl, st_ref)
        r = _heads(_silu(r_ref[sl, :]))
        _store_heads(o_ref, sl, [_rms_norm(o, ng) * rh for o, rh in zip(outs, r)])
        return carry

    lax.fori_loop(0, q_ref.shape[0] // rows, body, 0, unroll=CHUNK_UNROLL // PAIR)


def _hgrn_kernel(q_ref, f_ref, v_ref, r_ref, lb_ref, ng_ref, sums_ref, lvl_ref, o_ref, st_ref):
    @pl.when(pl.program_id(1) == 0)
    def _():
        st_ref[...] = jnp.zeros_like(st_ref)

    lvl = lvl_ref[...]
    ng = ng_ref[...]
    lb = lb_ref[...]
    lb_floor = jnp.maximum(lb, LB_EPS)
    rows = PAIR * CHUNK

    def body(ci, carry):
        sl = pl.ds(pl.multiple_of(ci * rows, rows), rows)
        f_pre = f_ref[sl, :]
        t = jnp.exp(-jnp.abs(f_pre))
        r = 1.0 / (1.0 + t)
        sig_pos = jnp.where(f_pre >= 0, r, t * r)
        sig_neg = jnp.where(f_pre >= 0, t * r, r)
        g = jnp.log(lb_floor + (1.0 - lb) * sig_pos)
        k = (1.0 - lb) * sig_neg
        q = _silu(q_ref[sl, :]) * HGRN_DK ** -0.5
        outs = _gla_chunks(_heads(q), _heads(k), _heads(v_ref[sl, :]), _heads(g), sums_ref, lvl, st_ref)
        r = _heads(_silu(r_ref[sl, :]))
        _store_heads(o_ref, sl, [_rms_norm(o, ng) * rh for o, rh in zip(outs, r)])
        return carry

    lax.fori_loop(0, q_ref.shape[0] // rows, body, 0, unroll=CHUNK_UNROLL // PAIR)


def _group_block(tt, nt, idx):
    return pl.BlockSpec((tt, GW), lambda b, t: (b * nt + t, idx))


def _const_block(shape):
    return pl.BlockSpec(shape, lambda *_: tuple(0 for _ in shape))


def gla_mixer(proj, gate_w, gate_b, norm_g, *, batch, tt=512):
    t = proj.shape[0]
    s = t // batch
    tt = min(tt, s)
    nt = s // tt
    sums, lvl = _gla_tables()
    gb = functools.partial(_group_block, tt, nt)
    return pl.pallas_call(
        _gla_kernel,
        grid=(batch, nt),
        in_specs=[gb(0), gb(1), gb(2), gb(3),
                  pl.BlockSpec((tt, LANES), lambda b, t: (b * nt + t, 4 * HEADS)),
                  _const_block((LANES, GW)), _const_block((1, GW)),
                  _const_block((1, LANES)), _const_block(sums.shape), _const_block(lvl.shape)],
        out_specs=gb(0),
        out_shape=jax.ShapeDtypeStruct((t, GW), BF16),
        scratch_shapes=[pltpu.VMEM((HEADS, LANES, LANES), F32)],
        compiler_params=_cparams(("parallel", "arbitrary")),
        name="gla",
    )(proj, proj, proj, proj, proj, gate_w, gate_b, norm_g.reshape(1, LANES),
      jnp.asarray(sums, BF16), jnp.asarray(lvl))


def hgrn_mixer(proj, lower_bound, norm_g, *, batch, tt=512):
    t = proj.shape[0]
    s = t // batch
    tt = min(tt, s)
    nt = s // tt
    sums, lvl = _gla_tables()
    gb = functools.partial(_group_block, tt, nt)
    return pl.pallas_call(
        _hgrn_kernel,
        grid=(batch, nt),
        in_specs=[gb(0), gb(1), gb(2), gb(3),
                  _const_block((1, GW)),
                  _const_block((1, LANES)), _const_block(sums.shape), _const_block(lvl.shape)],
        out_specs=gb(0),
        out_shape=jax.ShapeDtypeStruct((t, GW), BF16),
        scratch_shapes=[pltpu.VMEM((HEADS, LANES, LANES), F32)],
        compiler_params=_cparams(("parallel", "arbitrary")),
        name="hgrn",
    )(proj, proj, proj, proj, lower_bound.reshape(1, GW), norm_g.reshape(1, LANES),
      jnp.asarray(sums, BF16), jnp.asarray(lvl))


def _mlstm_kernel(q_ref, k_ref, v_ref, og_ref, if_ref, cw_q_ref, cw_k_ref, gb_ref, o_ref,
                  st_ref, m_ref, qx_ref, kx_ref):
    tt = q_ref.shape[0]

    @pl.when(pl.program_id(1) == 0)
    def _():
        st_ref[...] = jnp.zeros_like(st_ref)
        m_ref[...] = jnp.zeros_like(m_ref)
        qx_ref[0:CONV_PAD, :] = jnp.zeros((CONV_PAD, GW), F32)
        kx_ref[0:CONV_PAD, :] = jnp.zeros((CONV_PAD, GW), F32)

    qx_ref[CONV_PAD:CONV_PAD + tt, :] = q_ref[...]
    kx_ref[CONV_PAD:CONV_PAD + tt, :] = k_ref[...]

    c = CHUNK
    rows = PAIR * c

    def conv_silu(ext, cw, r0):
        xw = ext[pl.ds(r0, rows + CONV_PAD), :]
        acc = xw[CONV_PAD - (CONV_WIDTH - 1):CONV_PAD - (CONV_WIDTH - 1) + rows] * cw[0:1, :]
        for w in range(1, CONV_WIDTH):
            acc = acc + xw[CONV_PAD - (CONV_WIDTH - 1) + w:CONV_PAD - (CONV_WIDTH - 1) + w + rows] * cw[w:w + 1, :]
        return _silu(acc)

    row = lax.broadcasted_iota(jnp.int32, (rows, rows), 0)
    col = lax.broadcasted_iota(jnp.int32, (rows, rows), 1)
    causal = (col <= row) & (col >= (row // c) * c)
    eye = col == row
    tril3 = jnp.concatenate([jnp.where(causal, 1.0, 0.0).astype(BF16)] * 3, axis=1)
    chunk_of_row = lax.broadcasted_iota(jnp.int32, (rows, 1), 0) // c
    ones_col = jnp.where(lax.broadcasted_iota(jnp.int32, (rows, LANES), 1) == 0, 1.0, 0.0)
    gbias = gb_ref[...]

    def body(pi, carry):
        r0 = pl.multiple_of(pi * rows, rows)
        sl = pl.ds(r0, rows)
        gates = if_ref[sl, :] + gbias
        cum_all = _dot(tril3, _split3(_log_sigmoid(gates)))
        qc = _heads(conv_silu(qx_ref, cw_q_ref, r0))
        kc = _heads(conv_silu(kx_ref, cw_k_ref, r0) * MLSTM_DK ** -0.5)
        vs = _heads(v_ref[sl, :])
        og = _heads(_sigmoid(og_ref[sl, :]))
        outs = []
        for h in range(HEADS):
            i_col = gates[:, h:h + 1]
            cum = cum_all[:, HEADS + h:HEADS + h + 1]
            a_col = i_col - cum
            a_row = jnp.sum(jnp.where(eye, a_col, 0.0), axis=0, keepdims=True)
            m_in = [m_ref[h]]
            cum_last, lli = [], []
            for p in range(PAIR):
                cum_last.append(cum[(p + 1) * c - 1:(p + 1) * c, :])
                lli.append(cum_last[p] + a_col[p * c:(p + 1) * c])
                m_in.append(jnp.maximum(cum_last[p] + m_in[p], jnp.max(lli[p], axis=0, keepdims=True)))
            m_st = m_in[0]
            for p in range(1, PAIR):
                m_st = jnp.where(chunk_of_row >= p, m_in[p], m_st)
            log_intra = jnp.where(causal, cum + a_row, MASK_NEG)
            log_inter = cum + m_st
            m_t = jnp.maximum(log_inter, jnp.max(log_intra, axis=-1, keepdims=True))
            w_inter = jnp.exp(log_inter - m_t)
            w_intra = jnp.where(causal, jnp.exp(log_intra - m_t), 0.0)
            qb = qc[h].astype(BF16)
            kf = kc[h]
            v_ext = jnp.concatenate([vs[h], ones_col], axis=-1).astype(BF16)
            scores = _dot_nt(qb, kf.astype(BF16)) * w_intra
            nd_intra = _dot(scores.astype(BF16), v_ext)
            st = st_ref[h]
            nd_inter = []
            for p in range(PAIR):
                rs = slice(p * c, (p + 1) * c)
                nd_inter.append(_dot(qb[rs], st.astype(BF16)))
                wk = jnp.exp(lli[p] - m_in[p + 1])
                st = (jnp.exp(cum_last[p] + m_in[p] - m_in[p + 1]) * st
                      + _dot_tn((kf[rs] * wk).astype(BF16), v_ext[rs]))
            st_ref[h] = st
            m_ref[h] = m_in[PAIR]
            nd = w_inter * jnp.concatenate(nd_inter, axis=0) + nd_intra
            den = jnp.maximum(jnp.abs(nd[:, LANES:LANES + 1]), jnp.exp(-m_t))
            outs.append(nd[:, :LANES] / den * og[h])
        _store_heads(o_ref, sl, outs)
        return carry

    lax.fori_loop(0, tt // rows, body, 0, unroll=CHUNK_UNROLL // PAIR)
    qx_ref[0:CONV_PAD, :] = qx_ref[tt:tt + CONV_PAD, :]
    kx_ref[0:CONV_PAD, :] = kx_ref[tt:tt + CONV_PAD, :]


def mlstm_mixer(proj, conv_w_q, conv_w_k, gate_b, *, batch, tt=512):
    t = proj.shape[0]
    s = t // batch
    tt = min(tt, s)
    nt = s // tt
    gb = functools.partial(_group_block, tt, nt)
    return pl.pallas_call(
        _mlstm_kernel,
        grid=(batch, nt),
        in_specs=[gb(0), gb(1), gb(2), gb(3),
                  pl.BlockSpec((tt, LANES), lambda b, t: (b * nt + t, 4 * HEADS)),
                  _const_block((CONV_WIDTH, GW)), _const_block((CONV_WIDTH, GW)),
                  _const_block((1, LANES))],
        out_specs=gb(0),
        out_shape=jax.ShapeDtypeStruct((t, GW), BF16),
        scratch_shapes=[pltpu.VMEM((HEADS, LANES, 2 * LANES), F32), pltpu.VMEM((HEADS, 1, 1), F32),
                        pltpu.VMEM((CONV_PAD + tt, GW), F32), pltpu.VMEM((CONV_PAD + tt, GW), F32)],
        compiler_params=_cparams(("parallel", "arbitrary")),
        name="mlstm",
    )(proj, proj, proj, proj, proj, conv_w_q, conv_w_k, gate_b)


LOG2E = math.log2(math.e)


def _diff_attn_kernel(sc_ref, q_ref, k_ref, vt_ref, bias_ref, ng_ref, o_ref,
                      s_ref, p_ref, mx_ref, m_ref, c_ref, a_ref):
    qi = pl.program_id(2)
    q = q_ref[...].astype(F32) * (DIFF_DK ** -0.5 * LOG2E)
    lane = lax.broadcasted_iota(jnp.int32, q.shape, 1)
    qs = (jnp.where(lane < DIFF_DK, q, 0.0).astype(BF16), jnp.where(lane >= DIFF_DK, q, 0.0).astype(BF16))
    m_ref[...] = jnp.full(m_ref.shape, MASK_NEG, F32)
    c_ref[...] = jnp.ones_like(c_ref)
    a_ref[...] = jnp.zeros_like(a_ref)
    p_ref[1] = jnp.zeros(p_ref.shape[1:], BF16)
    tk = vt_ref.shape[2]
    ones_rows = jnp.where(lax.broadcasted_iota(jnp.int32, (BF16_SUBLANES, tk), 0) == 0, 1.0, 0.0).astype(BF16)

    def scores(kt, slot):
        kk = k_ref[pl.ds(pl.multiple_of(kt * tk, tk), tk), :]
        tile = jnp.minimum(qi - kt, 2)
        for mp in range(2):
            s = _dot_nt(kk, qs[mp]) + bias_ref[tile]
            s_ref[slot, mp] = s
            mx_ref[slot, mp] = jnp.max(s, axis=0, keepdims=True)

    def values(kt, slot):
        vt = jnp.concatenate([vt_ref[kt], ones_rows], axis=0)
        for mp in range(2):
            a_ref[mp] = a_ref[mp] * c_ref[mp] + _dot(vt, p_ref[slot, mp])

    scores(0, 0)

    def step(ki, cur):
        nxt = 1 - cur
        scores(jnp.minimum(ki + 1, qi), nxt)
        values(jnp.maximum(ki - 1, 0), nxt)
        for mp in range(2):
            m_old = m_ref[mp]
            m_new = jnp.maximum(m_old, mx_ref[cur, mp])
            p_ref[cur, mp] = jnp.exp2(s_ref[cur, mp] - m_new).astype(BF16)
            c_ref[mp] = jnp.exp2(m_old - m_new)
            m_ref[mp] = m_new

    def pair(j, carry):
        step(2 * j, 0)
        step(2 * j + 1, 1)
        return carry

    n_tiles = qi + 1
    lax.fori_loop(0, n_tiles // 2, pair, 0)

    @pl.when(lax.rem(n_tiles, 2) == 1)
    def _():
        step(qi, 0)

    values(qi, lax.rem(qi, 2))
    lam = sc_ref[0]
    dv = o_ref.shape[1]
    out_t = (a_ref[0, 0:dv] / a_ref[0, dv:dv + 1] - lam * (a_ref[1, 0:dv] / a_ref[1, dv:dv + 1]))
    out_t = out_t * lax.rsqrt(jnp.mean(out_t * out_t, axis=0, keepdims=True) + LN_EPS)
    o_ref[...] = (out_t.T * ng_ref[...] * sc_ref[1]).astype(o_ref.dtype)


def diff_attn_mixer(qk, vt, bias_tiles, scalars, norm_g, *, batch, tq):
    t = qk.shape[0]
    s = t // batch
    nt = s // tq
    dve = LANES + BF16_SUBLANES
    return pl.pallas_call(
        _diff_attn_kernel,
        grid=(batch, HEADS, nt),
        in_specs=[pl.BlockSpec(memory_space=pltpu.SMEM),
                  pl.BlockSpec((tq, LANES), lambda b, h, t: (b * nt + t, h)),
                  pl.BlockSpec((s, LANES), lambda b, h, t: (b, HEADS + h)),
                  pl.BlockSpec((None, nt, LANES, tq), lambda b, h, t: (b * HEADS + h, 0, 0, 0)),
                  pl.BlockSpec((None, 3, tq, tq), lambda b, h, t: (h, 0, 0, 0)),
                  _const_block((1, LANES))],
        out_specs=pl.BlockSpec((tq, LANES), lambda b, h, t: (b * nt + t, h)),
        out_shape=jax.ShapeDtypeStruct((t, HEADS * LANES), BF16),
        scratch_shapes=[pltpu.VMEM((2, 2, tq, tq), F32), pltpu.VMEM((2, 2, tq, tq), BF16),
                        pltpu.VMEM((2, 2, 1, tq), F32), pltpu.VMEM((2, 1, tq), F32), pltpu.VMEM((2, 1, tq), F32),
                        pltpu.VMEM((2, dve, tq), F32)],
        compiler_params=_cparams(("parallel", "parallel", "arbitrary")),
        name="diff_attn",
    )(scalars, qk, qk, vt, bias_tiles, norm_g.reshape(1, LANES))


def _t5_bucket_map(rel):
    n = np.maximum(rel, 0)
    max_exact = T5_BUCKETS // 2
    large = max_exact + (np.log(np.maximum(n, 1).astype(np.float32) / np.float32(max_exact))
                         / np.float32(math.log(T5_MAX_DIST / max_exact)) * (T5_BUCKETS - max_exact)).astype(np.int32)
    large = np.clip(large, max_exact, T5_BUCKETS - 1)
    return np.where(n < max_exact, n, large).astype(np.int32)


def t5_bias_tiles(t5_table, tq):
    assert tq >= T5_MAX_DIST
    j = np.arange(tq)[:, None]
    i = np.arange(tq)[None, :]
    rel = np.stack([d * tq + i - j for d in range(3)])
    bucket = jnp.asarray(_t5_bucket_map(rel).astype(np.int8))
    table = t5_table.astype(F32)
    tiles = jnp.zeros((HEADS,) + rel.shape, F32)
    for b in range(T5_BUCKETS):
        tiles = jnp.where(bucket[None] == b, table[b][:, None, None, None], tiles)
    return jnp.where(jnp.asarray(rel >= 0)[None], tiles * LOG2E, MASK_NEG)


def _xattn_kernel(xb_ref, wq_ref, k_ref, v_ref, o_ref):
    d = xb_ref.shape[1]
    hd = d // N_XHEADS
    q = _dot(xb_ref[...], wq_ref[...]).astype(BF16)
    for h in range(N_XHEADS):
        hs = slice(h * hd, (h + 1) * hd)
        s = _dot_nt(q[:, hs], k_ref[:, hs]) * hd ** -0.5
        s = s - jnp.max(s, axis=-1, keepdims=True)
        p = jnp.exp(s)
        p = p / jnp.sum(p, axis=-1, keepdims=True)
        o_ref[:, hs] = _dot(p.astype(BF16), v_ref[:, hs]).astype(o_ref.dtype)


def xattn(xb, wq, kv, *, batch, tm=512):
    t, d = xb.shape
    s = t // batch
    tm = min(tm, s)
    nt = s // tm
    n_mem = kv.shape[0] // batch
    return pl.pallas_call(
        _xattn_kernel,
        grid=(t // tm,),
        in_specs=[pl.BlockSpec((tm, d), lambda i: (i, 0)),
                  pl.BlockSpec((d, d), lambda i: (0, 0)),
                  pl.BlockSpec((n_mem, d), lambda i: (i // nt, 0)),
                  pl.BlockSpec((n_mem, d), lambda i: (i // nt, 1))],
        out_specs=pl.BlockSpec((tm, d), lambda i: (i, 0)),
        out_shape=jax.ShapeDtypeStruct((t, d), BF16),
        compiler_params=_cparams(("parallel",)),
        name="xattn",
    )(xb, wq, kv, kv)


def _pad_heads(w, dk):
    lead = w.shape[:-1]
    w = w.reshape(*lead, HEADS, dk)
    w = jnp.pad(w, [(0, 0)] * len(lead) + [(0, 0), (0, LANES - dk)])
    return w.reshape(*lead, HEADS * LANES)


def _pad_cols(w, n):
    return jnp.pad(w, [(0, 0)] * (w.ndim - 1) + [(0, n - w.shape[-1])])


def _split_in_proj(w_in):
    gw = HEADS * LANES
    sizes = (HEADS * GLA_DK, HEADS * GLA_DK, gw, GLA_GATE_RANK, gw,
             HEADS * HGRN_DK, HEADS * HGRN_DK, gw, gw,
             2 * HEADS * DIFF_DK, 2 * HEADS * DIFF_DK, gw,
             HEADS * MLSTM_DK, HEADS * MLSTM_DK, gw, 2 * HEADS, gw)
    assert sum(sizes) == w_in.shape[-1]
    (a_q, a_k, a_v, a_lr, a_r, b_q, b_f, b_i, b_g, c_q, c_k, c_v,
     d_q, d_k, d_v, d_if, d_o) = jnp.split(w_in, list(np.cumsum(sizes)[:-1]), axis=-1)
    w_gla = jnp.concatenate([_pad_heads(a_q, GLA_DK), _pad_heads(a_k, GLA_DK), a_v, a_r, _pad_cols(a_lr, LANES)], -1)
    w_hgrn = jnp.concatenate([b_q, b_f, b_i, b_g], -1)
    w_diff = jnp.concatenate([c_q, c_k, c_v], -1)
    w_mlstm = jnp.concatenate([_pad_heads(d_q, MLSTM_DK), _pad_heads(d_k, MLSTM_DK), d_v, d_o, _pad_cols(d_if, LANES)], -1)
    return [w.astype(BF16) for w in (w_gla, w_hgrn, w_diff, w_mlstm)]


def kernel(x, mem, ln_g, ln_b, ffn_w_in, ffn_w_out, w_in, w_out, gla_gate_w, gla_gate_b, gla_norm_g, hgrn_lb, hgrn_norm_g, diff_lambda, diff_norm_g, t5_table, mlstm_conv_w, mlstm_gate_b, xattn_w_q, xattn_w_kv, xattn_w_o):
    batch, seq, d = x.shape
    depth = ln_g.shape[0]
    alpha = (2 * depth) ** 0.25
    t = batch * seq

    sm = jax.nn.softmax(hgrn_lb.astype(F32), axis=0)
    lower_bounds = jnp.clip(jnp.cumsum(sm, axis=0) - sm[0], 0.0, 1.0 - 1e-6)
    tq = min(512, seq)
    bias_tiles = t5_bias_tiles(t5_table, tq)

    x = x.reshape(t, d)
    xb = x.astype(BF16)
    memb = mem.reshape(-1, d).astype(BF16)
    for l in range(depth):
        x, xb = ffn_ln(x, xb, ffn_w_in[l, 0].astype(BF16), ffn_w_out[l, 0].astype(BF16), ln_g[l, 0], ln_b[l, 0], alpha=alpha)

        w_gla, w_hgrn, w_diff, w_mlstm = _split_in_proj(w_in[l])
        o_a = gla_mixer(matmul(xb, w_gla),
                        _pad_heads(_pad_cols(gla_gate_w[l].T, LANES).T, GLA_DK).astype(BF16),
                        _pad_heads(gla_gate_b[l].reshape(1, -1), GLA_DK), gla_norm_g[l], batch=batch)
        o_b = hgrn_mixer(matmul(xb, w_hgrn), lower_bounds[l], hgrn_norm_g[l], batch=batch)
        lambda_init = 0.8 - 0.6 * math.exp(-0.3 * l)
        lq1, lk1, lq2, lk2 = [diff_lambda[l, j].astype(F32) for j in range(4)]
        lam = jnp.exp(jnp.sum(lq1 * lk1)) - jnp.exp(jnp.sum(lq2 * lk2)) + lambda_init
        p_c = matmul(xb, w_diff, out_dtype=BF16)
        vt_c = (p_c[:, 2 * GW:].reshape(batch, seq // tq, tq, HEADS, LANES).transpose(0, 3, 1, 4, 2)
                .reshape(batch * HEADS, seq // tq, LANES, tq))
        o_c = diff_attn_mixer(p_c, vt_c, bias_tiles,
                              jnp.stack([lam, jnp.asarray(1.0 - lambda_init, F32)]).astype(F32), diff_norm_g[l], batch=batch, tq=tq)
        cw = mlstm_conv_w[l]
        o_d = mlstm_mixer(matmul(xb, w_mlstm), _pad_heads(cw[:, :HEADS * MLSTM_DK], MLSTM_DK),
                          _pad_heads(cw[:, HEADS * MLSTM_DK:], MLSTM_DK),
                          _pad_cols(mlstm_gate_b[l].reshape(1, -1), LANES), batch=batch)
        x, xb = proj_ln(x, [o_a, o_b, o_c, o_d], w_out[l].astype(BF16), ln_g[l, 1], ln_b[l, 1], alpha=alpha)

        kv = matmul(memb, xattn_w_kv[l].astype(BF16), out_dtype=BF16, tn=1024)
        att = xattn(xb, xattn_w_q[l].astype(BF16), kv, batch=batch)
        x, xb = proj_ln(x, [att], xattn_w_o[l].astype(BF16), ln_g[l, 2], ln_b[l, 2], alpha=alpha)

        x, xb = ffn_ln(x, xb, ffn_w_in[l, 1].astype(BF16), ffn_w_out[l, 1].astype(BF16), ln_g[l, 3], ln_b[l, 3], alpha=alpha)
    return x.reshape(batch, seq, d)
```

```python
import functools
import math

import numpy as np
import jax
import jax.numpy as jnp
from jax import lax
from jax.experimental import pallas as pl
from jax.experimental.pallas import tpu as pltpu

F32 = jnp.float32
BF16 = jnp.bfloat16

HEADS = 4
N_XHEADS = 4
LANES = 128
BF16_SUBLANES = 16
GLA_DK = 64
HGRN_DK = 128
DIFF_DK = 64
MLSTM_DK = 64
GLA_GATE_RANK = 16
GLA_GATE_NORM = 16.0
CONV_WIDTH = 4
CHUNK = 64
T5_BUCKETS = 32
T5_MAX_DIST = 128
LN_EPS = 1e-5
LB_EPS = 1e-12
MASK_NEG = -1e30
CONV_PAD = 8
CHUNK_UNROLL = 4
PAIR = 2
QK_PACK = LANES // GLA_DK

VMEM_LIMIT = 56 * 1024 * 1024


def _cparams(sem):
    return pltpu.CompilerParams(dimension_semantics=sem, vmem_limit_bytes=VMEM_LIMIT)


def _sigmoid(x):
    return 1.0 / (1.0 + jnp.exp(-x))


def _silu(x):
    return x * _sigmoid(x)


def _log_sigmoid(x):
    return jnp.minimum(x, 0.0) - jnp.log1p(jnp.exp(-jnp.abs(x)))


def _layer_norm(y, g, b):
    mu = jnp.mean(y, axis=-1, keepdims=True)
    d = y - mu
    var = jnp.mean(d * d, axis=-1, keepdims=True)
    return d * lax.rsqrt(var + LN_EPS) * g + b


def _rms_norm(y, g):
    return y * lax.rsqrt(jnp.mean(y * y, axis=-1, keepdims=True) + LN_EPS) * g


def _dot(a, b):
    return jnp.dot(a, b, preferred_element_type=F32)


def _dot_nt(a, b):
    return lax.dot_general(a, b, (((1,), (1,)), ((), ())), preferred_element_type=F32)


def _dot_tn(a, b):
    return lax.dot_general(a, b, (((0,), (0,)), ((), ())), preferred_element_type=F32)


def _split3(g):
    hi = g.astype(BF16)
    r1 = g - hi.astype(F32)
    mid = r1.astype(BF16)
    lo = (r1 - mid.astype(F32)).astype(BF16)
    return jnp.concatenate([hi, mid, lo], axis=0)


def _ffn_ln_kernel(x_ref, xb_ref, wg_ref, wu_ref, wo_ref, g_ref, b_ref, o_ref, ob_ref, acc_ref, *, alpha):
    j = pl.program_id(1)

    @pl.when(j == 0)
    def _():
        acc_ref[...] = jnp.zeros_like(acc_ref)

    xb = xb_ref[...]
    gate = _dot(xb, wg_ref[...])
    up = _dot(xb, wu_ref[...])
    h = (_silu(gate) * up).astype(BF16)
    acc_ref[...] += _dot(h, wo_ref[...])

    @pl.when(j == pl.num_programs(1) - 1)
    def _():
        y = _layer_norm(alpha * x_ref[...] + 0.5 * acc_ref[...], g_ref[...], b_ref[...])
        o_ref[...] = y
        ob_ref[...] = y.astype(BF16)


def ffn_ln(x, xb, w_in, w_out, g, b, *, alpha, tm=512, tf=512):
    t, d = x.shape
    f = w_out.shape[0]
    tm, tf = min(tm, t), min(tf, f)
    nf = f // tf
    return pl.pallas_call(
        functools.partial(_ffn_ln_kernel, alpha=alpha),
        grid=(t // tm, nf),
        in_specs=[
            pl.BlockSpec((tm, d), lambda i, j: (i, 0)),
            pl.BlockSpec((tm, d), lambda i, j: (i, 0)),
            pl.BlockSpec((d, tf), lambda i, j: (0, j)),
            pl.BlockSpec((d, tf), lambda i, j: (0, j + nf)),
            pl.BlockSpec((tf, d), lambda i, j: (j, 0)),
            pl.BlockSpec((1, d), lambda i, j: (0, 0)),
            pl.BlockSpec((1, d), lambda i, j: (0, 0)),
        ],
        out_specs=[pl.BlockSpec((tm, d), lambda i, j: (i, 0)), pl.BlockSpec((tm, d), lambda i, j: (i, 0))],
        out_shape=[jax.ShapeDtypeStruct((t, d), F32), jax.ShapeDtypeStruct((t, d), BF16)],
        scratch_shapes=[pltpu.VMEM((tm, d), F32)],
        compiler_params=_cparams(("parallel", "arbitrary")),
        name="ffn_ln",
    )(x, xb, w_in, w_in, w_out, g.reshape(1, d), b.reshape(1, d))


def _matmul_kernel(x_ref, w_ref, o_ref):
    o_ref[...] = _dot(x_ref[...], w_ref[...]).astype(o_ref.dtype)


def matmul(xb, w, *, out_dtype=F32, tm=512, tn=None):
    t, k = xb.shape
    n = w.shape[1]
    tm = min(tm, t)
    tn = n if tn is None else min(tn, n)
    return pl.pallas_call(
        _matmul_kernel,
        grid=(n // tn, t // tm),
        in_specs=[pl.BlockSpec((tm, k), lambda j, i: (i, 0)), pl.BlockSpec((k, tn), lambda j, i: (0, j))],
        out_specs=pl.BlockSpec((tm, tn), lambda j, i: (i, j)),
        out_shape=jax.ShapeDtypeStruct((t, n), out_dtype),
        compiler_params=_cparams(("parallel", "parallel")),
        name="matmul",
    )(xb, w)


def _proj_ln_kernel(*refs, alpha, n_parts):
    x_ref = refs[0]
    parts = refs[1:1 + n_parts]
    w_ref, g_ref, b_ref, o_ref, ob_ref = refs[1 + n_parts:]
    kp = parts[0].shape[1]
    acc = alpha * x_ref[...]
    for p in range(n_parts):
        acc = acc + _dot(parts[p][...], w_ref[p * kp:(p + 1) * kp, :])
    y = _layer_norm(acc, g_ref[...], b_ref[...])
    o_ref[...] = y
    ob_ref[...] = y.astype(BF16)


def proj_ln(x, parts, w, g, b, *, alpha, tm=512):
    t, d = x.shape
    tm = min(tm, t)
    n_parts = len(parts)
    kp = parts[0].shape[1]
    return pl.pallas_call(
        functools.partial(_proj_ln_kernel, alpha=alpha, n_parts=n_parts),
        grid=(t // tm,),
        in_specs=[pl.BlockSpec((tm, d), lambda i: (i, 0))]
        + [pl.BlockSpec((tm, kp), lambda i: (i, 0)) for _ in parts]
        + [pl.BlockSpec((n_parts * kp, d), lambda i: (0, 0)),
           pl.BlockSpec((1, d), lambda i: (0, 0)), pl.BlockSpec((1, d), lambda i: (0, 0))],
        out_specs=[pl.BlockSpec((tm, d), lambda i: (i, 0)), pl.BlockSpec((tm, d), lambda i: (i, 0))],
        out_shape=[jax.ShapeDtypeStruct((t, d), F32), jax.ShapeDtypeStruct((t, d), BF16)],
        compiler_params=_cparams(("parallel",)),
        name="proj_ln",
    )(x, *parts, w, g.reshape(1, d), b.reshape(1, d))


_LEVELS = tuple(CHUNK >> (s + 1) for s in range(int(math.log2(CHUNK)) - 1))
_UNIT_LEVEL = len(_LEVELS)
_DIAG_LEVEL = _UNIT_LEVEL + 1
GW = HEADS * LANES


def _gla_tables():
    c = CHUNK
    i = np.arange(c)[:, None]
    t = np.arange(c)[None, :]
    blocks = [t <= i, t > i]
    for h in _LEVELS:
        lower = (i // h) % 2 == 1
        blocks.append(np.where(lower, (t >= (i // h) * h) & (t <= i), (t > i) & (t <= (i // h + 1) * h - 1)))
    sums = np.concatenate(blocks, axis=0).astype(np.float32)
    sums = np.concatenate([sums] * 3, axis=1)
    msb = np.floor(np.log2(np.maximum(i ^ t, 1))).astype(np.int32)
    lvl = np.where(t < i, _UNIT_LEVEL - msb, np.where(t == i, _DIAG_LEVEL, -1)).astype(np.int32)
    lvl_pair = np.full((PAIR * c, PAIR * c), -1, np.int32)
    for p in range(PAIR):
        lvl_pair[p * c:(p + 1) * c, p * c:(p + 1) * c] = lvl
    return sums, lvl_pair


def _gla_chunks(q_tiles, k_tiles, vs, g_tiles, sums_ref, lvl, st_ref):
    c = CHUNK
    n_tiles = len(q_tiles)
    per_tile = HEADS // n_tiles
    rows = q_tiles[0].shape[0]
    g_all = jnp.concatenate(g_tiles, axis=-1)
    g_wide = jnp.concatenate([g_all[p * c:(p + 1) * c] for p in range(PAIR)], axis=-1)
    e_all = _dot(sums_ref[...], _split3(g_wide))
    head_of_lane = lax.broadcasted_iota(jnp.int32, (rows, LANES), 1) // (LANES // per_tile)
    outs = [None] * HEADS
    for t, (q, k, g) in enumerate(zip(q_tiles, k_tiles, g_tiles)):
        e = [e_all[:, (p * n_tiles + t) * LANES:(p * n_tiles + t + 1) * LANES] for p in range(PAIR)]
        block = lambda b: jnp.concatenate([ep[b * c:(b + 1) * c] for ep in e], axis=0)
        qm = [q if per_tile == 1 else jnp.where(head_of_lane == j, q, 0.0) for j in range(per_tile)]
        attn = [jnp.where(lvl == _DIAG_LEVEL, jnp.sum(qj * k, axis=-1, keepdims=True), 0.0) for qj in qm]
        x = jnp.exp(g)
        kx = k.astype(BF16)
        attn = [jnp.where(lvl == _UNIT_LEVEL, _dot_nt((qj * x).astype(BF16), kx), a) for qj, a in zip(qm, attn)]
        for l in range(len(_LEVELS)):
            x = jnp.exp(block(2 + l))
            kx = (k * x).astype(BF16)
            attn = [jnp.where(lvl == l, _dot_nt((qj * x).astype(BF16), kx), a) for qj, a in zip(qm, attn)]
        x_in = jnp.exp(block(0))
        k_out = (k * jnp.exp(block(1))).astype(BF16)
        for j in range(per_tile):
            h = t * per_tile + j
            vb = vs[h].astype(BF16)
            o_intra = _dot(attn[j].astype(BF16), vb)
            q_in = (qm[j] * x_in).astype(BF16)
            st = st_ref[h]
            o_inter = []
            for p in range(PAIR):
                rs = slice(p * c, (p + 1) * c)
                o_inter.append(_dot_nt(q_in[rs], st.astype(BF16)))
                st = st * jnp.exp(e[p][c - 1:c]) + _dot_tn(vb[rs], k_out[rs])
            st_ref[h] = st
            outs[h] = o_intra + jnp.concatenate(o_inter, axis=0)
    return outs


def _tiles(x):
    return [x[:, t * LANES:(t + 1) * LANES] for t in range(x.shape[1] // LANES)]


def _heads(x):
    return [x[:, h * LANES:(h + 1) * LANES] for h in range(HEADS)]


def _store_heads(o_ref, sl, outs):
    for h, o in enumerate(outs):
        o_ref[sl, h * LANES:(h + 1) * LANES] = o.astype(o_ref.dtype)


def _gla_kernel(q_ref, k_ref, v_ref, r_ref, lr_ref, gw_ref, gb_ref, ng_ref, sums_ref, lvl_ref, o_ref, st_ref):
    @pl.when(pl.program_id(1) == 0)
    def _():
        st_ref[...] = jnp.zeros_like(st_ref)

    lvl = lvl_ref[...]
    ng = ng_ref[...]
    rows = PAIR * CHUNK

    def body(ci, carry):
        sl = pl.ds(pl.multiple_of(ci * rows, rows), rows)
        g = _log_sigmoid(_dot(lr_ref[sl, :].astype(BF16), gw_ref[...]) + gb_ref[...]) / GLA_GATE_NORM
        outs = _gla_chunks(_tiles(q_ref[sl, :] * GLA_DK ** -0.5), _tiles(k_ref[sl, :]), _heads(v_ref[sl, :]),
                           _tiles(g), sums_ref, lvl, st_ref)
        r = _heads(_silu(r_ref[sl, :]))
        _store_heads(o_ref, sl, [_rms_norm(o, ng) * rh for o, rh in zip(outs, r)])
        return carry

    lax.fori_loop(0, q_ref.shape[0] // rows, body, 0, unroll=CHUNK_UNROLL // PAIR)


def _hgrn_kernel(q_ref, f_ref, v_ref, r_ref, lb_ref, ng_ref, sums_ref, lvl_ref, o_ref, st_ref):
    @pl.when(pl.program_id(1) == 0)
    def _():
        st_ref[...] = jnp.zeros_like(st_ref)

    lvl = lvl_ref[...]
    ng = ng_ref[...]
    lb = lb_ref[...]
    lb_floor = jnp.maximum(lb, LB_EPS)
    rows = PAIR * CHUNK

    def body(ci, carry):
        sl = pl.ds(pl.multiple_of(ci * rows, rows), rows)
        f_pre = f_ref[sl, :]
        t = jnp.exp(-jnp.abs(f_pre))
        r = 1.0 / (1.0 + t)
        sig_pos = jnp.where(f_pre >= 0, r, t * r)
        sig_neg = jnp.where(f_pre >= 0, t * r, r)
        g = jnp.log(lb_floor + (1.0 - lb) * sig_pos)
        k = (1.0 - lb) * sig_neg
        q = _silu(q_ref[sl, :]) * HGRN_DK ** -0.5
        outs = _gla_chunks(_tiles(q), _tiles(k), _heads(v_ref[sl, :]), _tiles(g), sums_ref, lvl, st_ref)
        r = _heads(_silu(r_ref[sl, :]))
        _store_heads(o_ref, sl, [_rms_norm(o, ng) * rh for o, rh in zip(outs, r)])
        return carry

    lax.fori_loop(0, q_ref.shape[0] // rows, body, 0, unroll=CHUNK_UNROLL // PAIR)


def _group_block(tt, nt, idx):
    return pl.BlockSpec((tt, GW), lambda b, t: (b * nt + t, idx))


def _const_block(shape):
    return pl.BlockSpec(shape, lambda *_: tuple(0 for _ in shape))


def _packed_specs(tt, nt):
    qkw = GW // QK_PACK
    row = lambda b, t: b * nt + t
    return [pl.BlockSpec((tt, qkw), lambda b, t: (row(b, t), 0)),
            pl.BlockSpec((tt, qkw), lambda b, t: (row(b, t), 1)),
            pl.BlockSpec((tt, GW), lambda b, t: (row(b, t), (2 * qkw) // GW)),
            pl.BlockSpec((tt, GW), lambda b, t: (row(b, t), (2 * qkw) // GW + 1)),
            pl.BlockSpec((tt, LANES), lambda b, t: (row(b, t), (2 * qkw + 2 * GW) // LANES))]


def gla_mixer(proj, gate_w, gate_b, norm_g, *, batch, tt=512):
    t = proj.shape[0]
    s = t // batch
    tt = min(tt, s)
    nt = s // tt
    sums, lvl = _gla_tables()
    gb = functools.partial(_group_block, tt, nt)
    return pl.pallas_call(
        _gla_kernel,
        grid=(batch, nt),
        in_specs=_packed_specs(tt, nt)
        + [_const_block(gate_w.shape), _const_block(gate_b.shape),
           _const_block((1, LANES)), _const_block(sums.shape), _const_block(lvl.shape)],
        out_specs=gb(0),
        out_shape=jax.ShapeDtypeStruct((t, GW), BF16),
        scratch_shapes=[pltpu.VMEM((HEADS, LANES, LANES), F32)],
        compiler_params=_cparams(("parallel", "arbitrary")),
        name="gla",
    )(proj, proj, proj, proj, proj, gate_w, gate_b, norm_g.reshape(1, LANES),
      jnp.asarray(sums, BF16), jnp.asarray(lvl))


def hgrn_mixer(proj, lower_bound, norm_g, *, batch, tt=512):
    t = proj.shape[0]
    s = t // batch
    tt = min(tt, s)
    nt = s // tt
    sums, lvl = _gla_tables()
    gb = functools.partial(_group_block, tt, nt)
    return pl.pallas_call(
        _hgrn_kernel,
        grid=(batch, nt),
        in_specs=[gb(0), gb(1), gb(2), gb(3),
                  _const_block((1, GW)),
                  _const_block((1, LANES)), _const_block(sums.shape), _const_block(lvl.shape)],
        out_specs=gb(0),
        out_shape=jax.ShapeDtypeStruct((t, GW), BF16),
        scratch_shapes=[pltpu.VMEM((HEADS, LANES, LANES), F32)],
        compiler_params=_cparams(("parallel", "arbitrary")),
        name="hgrn",
    )(proj, proj, proj, proj, lower_bound.reshape(1, GW), norm_g.reshape(1, LANES),
      jnp.asarray(sums, BF16), jnp.asarray(lvl))


def _mlstm_kernel(q_ref, k_ref, v_ref, og_ref, if_ref, cw_q_ref, cw_k_ref, gb_ref, o_ref,
                  st_ref, m_ref, qx_ref, kx_ref):
    tt = q_ref.shape[0]

    @pl.when(pl.program_id(1) == 0)
    def _():
        st_ref[...] = jnp.zeros_like(st_ref)
        m_ref[...] = jnp.zeros_like(m_ref)
        qx_ref[0:CONV_PAD, :] = jnp.zeros((CONV_PAD, qx_ref.shape[1]), F32)
        kx_ref[0:CONV_PAD, :] = jnp.zeros((CONV_PAD, kx_ref.shape[1]), F32)

    qx_ref[CONV_PAD:CONV_PAD + tt, :] = q_ref[...]
    kx_ref[CONV_PAD:CONV_PAD + tt, :] = k_ref[...]

    c = CHUNK
    rows = PAIR * c

    def conv_silu(ext, cw, r0):
        xw = ext[pl.ds(r0, rows + CONV_PAD), :]
        acc = xw[CONV_PAD - (CONV_WIDTH - 1):CONV_PAD - (CONV_WIDTH - 1) + rows] * cw[0:1, :]
        for w in range(1, CONV_WIDTH):
            acc = acc + xw[CONV_PAD - (CONV_WIDTH - 1) + w:CONV_PAD - (CONV_WIDTH - 1) + w + rows] * cw[w:w + 1, :]
        return _silu(acc)

    row = lax.broadcasted_iota(jnp.int32, (rows, rows), 0)
    col = lax.broadcasted_iota(jnp.int32, (rows, rows), 1)
    causal = (col <= row) & (col >= (row // c) * c)
    eye = col == row
    tril3 = jnp.concatenate([jnp.where(causal, 1.0, 0.0).astype(BF16)] * 3, axis=1)
    chunk_of_row = lax.broadcasted_iota(jnp.int32, (rows, 1), 0) // c
    head_of_lane = lax.broadcasted_iota(jnp.int32, (rows, LANES), 1) // MLSTM_DK
    ones_col = jnp.where(lax.broadcasted_iota(jnp.int32, (rows, LANES), 1) == 0, 1.0, 0.0)
    gbias = gb_ref[...]

    def body(pi, carry):
        r0 = pl.multiple_of(pi * rows, rows)
        sl = pl.ds(r0, rows)
        gates = if_ref[sl, :] + gbias
        cum_all = _dot(tril3, _split3(_log_sigmoid(gates)))
        qc = _tiles(conv_silu(qx_ref, cw_q_ref, r0))
        kc = _tiles(conv_silu(kx_ref, cw_k_ref, r0) * MLSTM_DK ** -0.5)
        vs = _heads(v_ref[sl, :])
        og = _heads(_sigmoid(og_ref[sl, :]))
        outs = []
        for h in range(HEADS):
            i_col = gates[:, h:h + 1]
            cum = cum_all[:, HEADS + h:HEADS + h + 1]
            a_col = i_col - cum
            a_row = jnp.sum(jnp.where(eye, a_col, 0.0), axis=0, keepdims=True)
            m_in = [m_ref[h]]
            cum_last, lli = [], []
            for p in range(PAIR):
                cum_last.append(cum[(p + 1) * c - 1:(p + 1) * c, :])
                lli.append(cum_last[p] + a_col[p * c:(p + 1) * c])
                m_in.append(jnp.maximum(cum_last[p] + m_in[p], jnp.max(lli[p], axis=0, keepdims=True)))
            m_st = m_in[0]
            for p in range(1, PAIR):
                m_st = jnp.where(chunk_of_row >= p, m_in[p], m_st)
            log_intra = jnp.where(causal, cum + a_row, MASK_NEG)
            log_inter = cum + m_st
            m_t = jnp.maximum(log_inter, jnp.max(log_intra, axis=-1, keepdims=True))
            w_inter = jnp.exp(log_inter - m_t)
            w_intra = jnp.where(causal, jnp.exp(log_intra - m_t), 0.0)
            qb = jnp.where(head_of_lane == h % QK_PACK, qc[h // QK_PACK], 0.0).astype(BF16)
            kf = kc[h // QK_PACK]
            v_ext = jnp.concatenate([vs[h], ones_col], axis=-1).astype(BF16)
            scores = _dot_nt(qb, kf.astype(BF16)) * w_intra
            nd_intra = _dot(scores.astype(BF16), v_ext)
            st = st_ref[h]
            nd_inter = []
            for p in range(PAIR):
                rs = slice(p * c, (p + 1) * c)
                nd_inter.append(_dot(qb[rs], st.astype(BF16)))
                wk = jnp.exp(lli[p] - m_in[p + 1])
                st = (jnp.exp(cum_last[p] + m_in[p] - m_in[p + 1]) * st
                      + _dot_tn((kf[rs] * wk).astype(BF16), v_ext[rs]))
            st_ref[h] = st
            m_ref[h] = m_in[PAIR]
            nd = w_inter * jnp.concatenate(nd_inter, axis=0) + nd_intra
            den = jnp.maximum(jnp.abs(nd[:, LANES:LANES + 1]), jnp.exp(-m_t))
            outs.append(nd[:, :LANES] / den * og[h])
        _store_heads(o_ref, sl, outs)
        return carry

    lax.fori_loop(0, tt // rows, body, 0, unroll=CHUNK_UNROLL // PAIR)
    qx_ref[0:CONV_PAD, :] = qx_ref[tt:tt + CONV_PAD, :]
    kx_ref[0:CONV_PAD, :] = kx_ref[tt:tt + CONV_PAD, :]


def mlstm_mixer(proj, conv_w_q, conv_w_k, gate_b, *, batch, tt=512):
    t = proj.shape[0]
    s = t // batch
    tt = min(tt, s)
    nt = s // tt
    gb = functools.partial(_group_block, tt, nt)
    return pl.pallas_call(
        _mlstm_kernel,
        grid=(batch, nt),
        in_specs=_packed_specs(tt, nt)
        + [_const_block(conv_w_q.shape), _const_block(conv_w_k.shape), _const_block((1, LANES))],
        out_specs=gb(0),
        out_shape=jax.ShapeDtypeStruct((t, GW), BF16),
        scratch_shapes=[pltpu.VMEM((HEADS, LANES, 2 * LANES), F32), pltpu.VMEM((HEADS, 1, 1), F32),
                        pltpu.VMEM((CONV_PAD + tt, GW // QK_PACK), F32),
                        pltpu.VMEM((CONV_PAD + tt, GW // QK_PACK), F32)],
        compiler_params=_cparams(("parallel", "arbitrary")),
        name="mlstm",
    )(proj, proj, proj, proj, proj, conv_w_q, conv_w_k, gate_b)


LOG2E = math.log2(math.e)


def _diff_attn_kernel(sc_ref, q_ref, k_ref, vt_ref, bias_ref, ng_ref, o_ref,
                      s_ref, p_ref, mx_ref, m_ref, c_ref, a_ref):
    qi = pl.program_id(2)
    q = q_ref[...].astype(F32) * (DIFF_DK ** -0.5 * LOG2E)
    lane = lax.broadcasted_iota(jnp.int32, q.shape, 1)
    qs = (jnp.where(lane < DIFF_DK, q, 0.0).astype(BF16), jnp.where(lane >= DIFF_DK, q, 0.0).astype(BF16))
    m_ref[...] = jnp.full(m_ref.shape, MASK_NEG, F32)
    c_ref[...] = jnp.ones_like(c_ref)
    a_ref[...] = jnp.zeros_like(a_ref)
    p_ref[1] = jnp.zeros(p_ref.shape[1:], BF16)
    tk = vt_ref.shape[2]
    ones_rows = jnp.where(lax.broadcasted_iota(jnp.int32, (BF16_SUBLANES, tk), 0) == 0, 1.0, 0.0).astype(BF16)

    def scores(kt, slot):
        kk = k_ref[pl.ds(pl.multiple_of(kt * tk, tk), tk), :]
        tile = jnp.minimum(qi - kt, 2)
        for mp in range(2):
            s = _dot_nt(kk, qs[mp]) + bias_ref[tile]
            s_ref[slot, mp] = s
            mx_ref[slot, mp] = jnp.max(s, axis=0, keepdims=True)

    def values(kt, slot):
        vt = jnp.concatenate([vt_ref[kt], ones_rows], axis=0)
        for mp in range(2):
            a_ref[mp] = a_ref[mp] * c_ref[mp] + _dot(vt, p_ref[slot, mp])

    scores(0, 0)

    def step(ki, cur):
        nxt = 1 - cur
        scores(jnp.minimum(ki + 1, qi), nxt)
        values(jnp.maximum(ki - 1, 0), nxt)
        for mp in range(2):
            m_old = m_ref[mp]
            m_new = jnp.maximum(m_old, mx_ref[cur, mp])
            p_ref[cur, mp] = jnp.exp2(s_ref[cur, mp] - m_new).astype(BF16)
            c_ref[mp] = jnp.exp2(m_old - m_new)
            m_ref[mp] = m_new

    def pair(j, carry):
        step(2 * j, 0)
        step(2 * j + 1, 1)
        return carry

    n_tiles = qi + 1
    lax.fori_loop(0, n_tiles // 2, pair, 0)

    @pl.when(lax.rem(n_tiles, 2) == 1)
    def _():
        step(qi, 0)

    values(qi, lax.rem(qi, 2))
    lam = sc_ref[0]
    dv = o_ref.shape[1]
    out_t = (a_ref[0, 0:dv] / a_ref[0, dv:dv + 1] - lam * (a_ref[1, 0:dv] / a_ref[1, dv:dv + 1]))
    out_t = out_t * lax.rsqrt(jnp.mean(out_t * out_t, axis=0, keepdims=True) + LN_EPS)
    o_ref[...] = (out_t.T * ng_ref[...] * sc_ref[1]).astype(o_ref.dtype)


def diff_attn_mixer(qk, vt, bias_tiles, scalars, norm_g, *, batch, tq):
    t = qk.shape[0]
    s = t // batch
    nt = s // tq
    dve = LANES + BF16_SUBLANES
    return pl.pallas_call(
        _diff_attn_kernel,
        grid=(batch, HEADS, nt),
        in_specs=[pl.BlockSpec(memory_space=pltpu.SMEM),
                  pl.BlockSpec((tq, LANES), lambda b, h, t: (b * nt + t, h)),
                  pl.BlockSpec((s, LANES), lambda b, h, t: (b, HEADS + h)),
                  pl.BlockSpec((None, nt, LANES, tq), lambda b, h, t: (b * HEADS + h, 0, 0, 0)),
                  pl.BlockSpec((None, 3, tq, tq), lambda b, h, t: (h, 0, 0, 0)),
                  _const_block((1, LANES))],
        out_specs=pl.BlockSpec((tq, LANES), lambda b, h, t: (b * nt + t, h)),
        out_shape=jax.ShapeDtypeStruct((t, HEADS * LANES), BF16),
        scratch_shapes=[pltpu.VMEM((2, 2, tq, tq), F32), pltpu.VMEM((2, 2, tq, tq), BF16),
                        pltpu.VMEM((2, 2, 1, tq), F32), pltpu.VMEM((2, 1, tq), F32), pltpu.VMEM((2, 1, tq), F32),
                        pltpu.VMEM((2, dve, tq), F32)],
        compiler_params=_cparams(("parallel", "parallel", "arbitrary")),
        name="diff_attn",
    )(scalars, qk, qk, vt, bias_tiles, norm_g.reshape(1, LANES))


def _t5_bucket_map(rel):
    n = np.maximum(rel, 0)
    max_exact = T5_BUCKETS // 2
    large = max_exact + (np.log(np.maximum(n, 1).astype(np.float32) / np.float32(max_exact))
                         / np.float32(math.log(T5_MAX_DIST / max_exact)) * (T5_BUCKETS - max_exact)).astype(np.int32)
    large = np.clip(large, max_exact, T5_BUCKETS - 1)
    return np.where(n < max_exact, n, large).astype(np.int32)


def t5_bias_tiles(t5_table, tq):
    assert tq >= T5_MAX_DIST
    j = np.arange(tq)[:, None]
    i = np.arange(tq)[None, :]
    rel = np.stack([d * tq + i - j for d in range(3)])
    bucket = jnp.asarray(_t5_bucket_map(rel).astype(np.int8))
    table = t5_table.astype(F32)
    tiles = jnp.zeros((HEADS,) + rel.shape, F32)
    for b in range(T5_BUCKETS):
        tiles = jnp.where(bucket[None] == b, table[b][:, None, None, None], tiles)
    return jnp.where(jnp.asarray(rel >= 0)[None], tiles * LOG2E, MASK_NEG)


def _xattn_kernel(xb_ref, wq_ref, k_ref, v_ref, o_ref):
    d = xb_ref.shape[1]
    hd = d // N_XHEADS
    q = _dot(xb_ref[...], wq_ref[...]).astype(BF16)
    for h in range(N_XHEADS):
        hs = slice(h * hd, (h + 1) * hd)
        s = _dot_nt(q[:, hs], k_ref[:, hs]) * hd ** -0.5
        s = s - jnp.max(s, axis=-1, keepdims=True)
        p = jnp.exp(s)
        p = p / jnp.sum(p, axis=-1, keepdims=True)
        o_ref[:, hs] = _dot(p.astype(BF16), v_ref[:, hs]).astype(o_ref.dtype)


def xattn(xb, wq, kv, *, batch, tm=512):
    t, d = xb.shape
    s = t // batch
    tm = min(tm, s)
    nt = s // tm
    n_mem = kv.shape[0] // batch
    return pl.pallas_call(
        _xattn_kernel,
        grid=(t // tm,),
        in_specs=[pl.BlockSpec((tm, d), lambda i: (i, 0)),
                  pl.BlockSpec((d, d), lambda i: (0, 0)),
                  pl.BlockSpec((n_mem, d), lambda i: (i // nt, 0)),
                  pl.BlockSpec((n_mem, d), lambda i: (i // nt, 1))],
        out_specs=pl.BlockSpec((tm, d), lambda i: (i, 0)),
        out_shape=jax.ShapeDtypeStruct((t, d), BF16),
        compiler_params=_cparams(("parallel",)),
        name="xattn",
    )(xb, wq, kv, kv)


def _pad_cols(w, n):
    return jnp.pad(w, [(0, 0)] * (w.ndim - 1) + [(0, n - w.shape[-1])])


def _split_in_proj(w_in):
    gw = HEADS * LANES
    sizes = (HEADS * GLA_DK, HEADS * GLA_DK, gw, GLA_GATE_RANK, gw,
             HEADS * HGRN_DK, HEADS * HGRN_DK, gw, gw,
             2 * HEADS * DIFF_DK, 2 * HEADS * DIFF_DK, gw,
             HEADS * MLSTM_DK, HEADS * MLSTM_DK, gw, 2 * HEADS, gw)
    assert sum(sizes) == w_in.shape[-1]
    (a_q, a_k, a_v, a_lr, a_r, b_q, b_f, b_i, b_g, c_q, c_k, c_v,
     d_q, d_k, d_v, d_if, d_o) = jnp.split(w_in, list(np.cumsum(sizes)[:-1]), axis=-1)
    w_gla = jnp.concatenate([a_q, a_k, a_v, a_r, _pad_cols(a_lr, LANES)], -1)
    w_hgrn = jnp.concatenate([b_q, b_f, b_i, b_g], -1)
    w_diff = jnp.concatenate([c_q, c_k, c_v], -1)
    w_mlstm = jnp.concatenate([d_q, d_k, d_v, d_o, _pad_cols(d_if, LANES)], -1)
    return [w.astype(BF16) for w in (w_gla, w_hgrn, w_diff, w_mlstm)]


def kernel(x, mem, ln_g, ln_b, ffn_w_in, ffn_w_out, w_in, w_out, gla_gate_w, gla_gate_b, gla_norm_g, hgrn_lb, hgrn_norm_g, diff_lambda, diff_norm_g, t5_table, mlstm_conv_w, mlstm_gate_b, xattn_w_q, xattn_w_kv, xattn_w_o):
    batch, seq, d = x.shape
    depth = ln_g.shape[0]
    alpha = (2 * depth) ** 0.25
    t = batch * seq

    sm = jax.nn.softmax(hgrn_lb.astype(F32), axis=0)
    lower_bounds = jnp.clip(jnp.cumsum(sm, axis=0) - sm[0], 0.0, 1.0 - 1e-6)
    tq = min(512, seq)
    bias_tiles = t5_bias_tiles(t5_table, tq)

    x = x.reshape(t, d)
    xb = x.astype(BF16)
    memb = mem.reshape(-1, d).astype(BF16)
    for l in range(depth):
        x, xb = ffn_ln(x, xb, ffn_w_in[l, 0].astype(BF16), ffn_w_out[l, 0].astype(BF16), ln_g[l, 0], ln_b[l, 0], alpha=alpha)

        w_gla, w_hgrn, w_diff, w_mlstm = _split_in_proj(w_in[l])
        o_a = gla_mixer(matmul(xb, w_gla), _pad_cols(gla_gate_w[l].T, LANES).T.astype(BF16),
                        gla_gate_b[l].reshape(1, -1), gla_norm_g[l], batch=batch)
        o_b = hgrn_mixer(matmul(xb, w_hgrn), lower_bounds[l], hgrn_norm_g[l], batch=batch)
        lambda_init = 0.8 - 0.6 * math.exp(-0.3 * l)
        lq1, lk1, lq2, lk2 = [diff_lambda[l, j].astype(F32) for j in range(4)]
        lam = jnp.exp(jnp.sum(lq1 * lk1)) - jnp.exp(jnp.sum(lq2 * lk2)) + lambda_init
        p_c = matmul(xb, w_diff, out_dtype=BF16)
        vt_c = (p_c[:, 2 * GW:].reshape(batch, seq // tq, tq, HEADS, LANES).transpose(0, 3, 1, 4, 2)
                .reshape(batch * HEADS, seq // tq, LANES, tq))
        o_c = diff_attn_mixer(p_c, vt_c, bias_tiles,
                              jnp.stack([lam, jnp.asarray(1.0 - lambda_init, F32)]).astype(F32), diff_norm_g[l], batch=batch, tq=tq)
        cw = mlstm_conv_w[l]
        o_d = mlstm_mixer(matmul(xb, w_mlstm), cw[:, :HEADS * MLSTM_DK], cw[:, HEADS * MLSTM_DK:],
                          _pad_cols(mlstm_gate_b[l].reshape(1, -1), LANES), batch=batch)
        x, xb = proj_ln(x, [o_a, o_b, o_c, o_d], w_out[l].astype(BF16), ln_g[l, 1], ln_b[l, 1], alpha=alpha)

        kv = matmul(memb, xattn_w_kv[l].astype(BF16), out_dtype=BF16, tn=1024)
        att = xattn(xb, xattn_w_q[l].astype(BF16), kv, batch=batch)
        x, xb = proj_ln(x, [att], xattn_w_o[l].astype(BF16), ln_g[l, 2], ln_b[l, 2], alpha=alpha)

        x, xb = ffn_ln(x, xb, ffn_w_in[l, 1].astype(BF16), ffn_w_out[l, 1].astype(BF16), ln_g[l, 3], ln_b[l, 3], alpha=alpha)
    return x.reshape(batch, seq, d)
```

```python
import functools
import math

import numpy as np
import jax
import jax.numpy as jnp
from jax import lax
from jax.experimental import pallas as pl
from jax.experimental.pallas import tpu as pltpu

F32 = jnp.float32
BF16 = jnp.bfloat16

HEADS = 4
N_XHEADS = 4
LANES = 128
BF16_SUBLANES = 16
GLA_DK = 64
HGRN_DK = 128
DIFF_DK = 64
MLSTM_DK = 64
GLA_GATE_RANK = 16
GLA_GATE_NORM = 16.0
CONV_WIDTH = 4
CHUNK = 64
T5_BUCKETS = 32
T5_MAX_DIST = 128
LN_EPS = 1e-5
LB_EPS = 1e-12
MASK_NEG = -1e30
CONV_PAD = 8
CHUNK_UNROLL = 4
PAIR = 2
QK_PACK = LANES // GLA_DK

VMEM_LIMIT = 56 * 1024 * 1024


def _cparams(sem):
    return pltpu.CompilerParams(dimension_semantics=sem, vmem_limit_bytes=VMEM_LIMIT)


def _sigmoid(x):
    return 1.0 / (1.0 + jnp.exp(-x))


def _silu(x):
    return x * _sigmoid(x)


def _log_sigmoid(x):
    return jnp.minimum(x, 0.0) - jnp.log1p(jnp.exp(-jnp.abs(x)))


def _layer_norm(y, g, b):
    mu = jnp.mean(y, axis=-1, keepdims=True)
    d = y - mu
    var = jnp.mean(d * d, axis=-1, keepdims=True)
    return d * lax.rsqrt(var + LN_EPS) * g + b


def _rms_norm(y, g):
    return y * lax.rsqrt(jnp.mean(y * y, axis=-1, keepdims=True) + LN_EPS) * g


def _dot(a, b):
    return jnp.dot(a, b, preferred_element_type=F32)


def _dot_nt(a, b):
    return lax.dot_general(a, b, (((1,), (1,)), ((), ())), preferred_element_type=F32)


def _dot_tn(a, b):
    return lax.dot_general(a, b, (((0,), (0,)), ((), ())), preferred_element_type=F32)


def _split3(g):
    hi = g.astype(BF16)
    r1 = g - hi.astype(F32)
    mid = r1.astype(BF16)
    lo = (r1 - mid.astype(F32)).astype(BF16)
    return jnp.concatenate([hi, mid, lo], axis=0)


ROW_SPLIT = 2


def _ffn_ln_kernel(x_ref, xb_ref, wg_ref, wu_ref, wo_ref, g_ref, b_ref, o_ref, ob_ref, acc_ref, *, alpha, nf):
    j = pl.program_id(1)
    tm = x_ref.shape[0]

    def hidden():
        xb = xb_ref[...]
        return (_silu(_dot(xb, wg_ref[...])) * _dot(xb, wu_ref[...])).astype(BF16)

    def finish(first):
        h = hidden()
        rows = tm // ROW_SPLIT
        for r in range(ROW_SPLIT):
            rs = slice(r * rows, (r + 1) * rows)
            y = _dot(h[rs], wo_ref[...])
            if not first:
                y = y + acc_ref[rs, :]
            y = _layer_norm(alpha * x_ref[rs, :] + 0.5 * y, g_ref[...], b_ref[...])
            o_ref[rs, :] = y
            ob_ref[rs, :] = y.astype(BF16)

    if nf == 1:
        finish(True)
        return

    @pl.when(j == 0)
    def _():
        acc_ref[...] = _dot(hidden(), wo_ref[...])

    @pl.when((j > 0) & (j < nf - 1))
    def _():
        acc_ref[...] += _dot(hidden(), wo_ref[...])

    @pl.when(j == nf - 1)
    def _():
        finish(False)


def ffn_ln(x, xb, w_in, w_out, g, b, *, alpha, tm=512, tf=512):
    t, d = x.shape
    f = w_out.shape[0]
    tm, tf = min(tm, t), min(tf, f)
    nf = f // tf
    return pl.pallas_call(
        functools.partial(_ffn_ln_kernel, alpha=alpha, nf=nf),
        grid=(t // tm, nf),
        in_specs=[
            pl.BlockSpec((tm, d), lambda i, j: (i, 0)),
            pl.BlockSpec((tm, d), lambda i, j: (i, 0)),
            pl.BlockSpec((d, tf), lambda i, j: (0, j)),
            pl.BlockSpec((d, tf), lambda i, j: (0, j + nf)),
            pl.BlockSpec((tf, d), lambda i, j: (j, 0)),
            pl.BlockSpec((1, d), lambda i, j: (0, 0)),
            pl.BlockSpec((1, d), lambda i, j: (0, 0)),
        ],
        out_specs=[pl.BlockSpec((tm, d), lambda i, j: (i, 0)), pl.BlockSpec((tm, d), lambda i, j: (i, 0))],
        out_shape=[jax.ShapeDtypeStruct((t, d), F32), jax.ShapeDtypeStruct((t, d), BF16)],
        scratch_shapes=[pltpu.VMEM((tm, d), F32)],
        compiler_params=_cparams(("parallel", "arbitrary")),
        name="ffn_ln",
    )(x, xb, w_in, w_in, w_out, g.reshape(1, d), b.reshape(1, d))


def _matmul_kernel(x_ref, w_ref, o_ref):
    o_ref[...] = _dot(x_ref[...], w_ref[...]).astype(o_ref.dtype)


def matmul(xb, w, *, out_dtype=F32, tm=512, tn=None):
    t, k = xb.shape
    n = w.shape[1]
    tm = min(tm, t)
    tn = n if tn is None else min(tn, n)
    return pl.pallas_call(
        _matmul_kernel,
        grid=(n // tn, t // tm),
        in_specs=[pl.BlockSpec((tm, k), lambda j, i: (i, 0)), pl.BlockSpec((k, tn), lambda j, i: (0, j))],
        out_specs=pl.BlockSpec((tm, tn), lambda j, i: (i, j)),
        out_shape=jax.ShapeDtypeStruct((t, n), out_dtype),
        compiler_params=_cparams(("parallel", "parallel")),
        name="matmul",
    )(xb, w)


def _proj_ln_kernel(*refs, alpha, n_parts):
    x_ref = refs[0]
    parts = refs[1:1 + n_parts]
    w_ref, g_ref, b_ref, o_ref, ob_ref = refs[1 + n_parts:]
    kp = parts[0].shape[1]
    rows = x_ref.shape[0] // ROW_SPLIT
    for r in range(ROW_SPLIT):
        rs = slice(r * rows, (r + 1) * rows)
        acc = alpha * x_ref[rs, :]
        for p in range(n_parts):
            acc = acc + _dot(parts[p][rs, :], w_ref[p * kp:(p + 1) * kp, :])
        y = _layer_norm(acc, g_ref[...], b_ref[...])
        o_ref[rs, :] = y
        ob_ref[rs, :] = y.astype(BF16)


def proj_ln(x, parts, w, g, b, *, alpha, tm=512):
    t, d = x.shape
    tm = min(tm, t)
    n_parts = len(parts)
    kp = parts[0].shape[1]
    return pl.pallas_call(
        functools.partial(_proj_ln_kernel, alpha=alpha, n_parts=n_parts),
        grid=(t // tm,),
        in_specs=[pl.BlockSpec((tm, d), lambda i: (i, 0))]
        + [pl.BlockSpec((tm, kp), lambda i: (i, 0)) for _ in parts]
        + [pl.BlockSpec((n_parts * kp, d), lambda i: (0, 0)),
           pl.BlockSpec((1, d), lambda i: (0, 0)), pl.BlockSpec((1, d), lambda i: (0, 0))],
        out_specs=[pl.BlockSpec((tm, d), lambda i: (i, 0)), pl.BlockSpec((tm, d), lambda i: (i, 0))],
        out_shape=[jax.ShapeDtypeStruct((t, d), F32), jax.ShapeDtypeStruct((t, d), BF16)],
        compiler_params=_cparams(("parallel",)),
        name="proj_ln",
    )(x, *parts, w, g.reshape(1, d), b.reshape(1, d))


_LEVELS = tuple(CHUNK >> (s + 1) for s in range(int(math.log2(CHUNK)) - 1))
_UNIT_LEVEL = len(_LEVELS)
_DIAG_LEVEL = _UNIT_LEVEL + 1
GW = HEADS * LANES


def _gla_tables():
    c = CHUNK
    i = np.arange(c)[:, None]
    t = np.arange(c)[None, :]
    blocks = [t <= i, t > i]
    for h in _LEVELS:
        lower = (i // h) % 2 == 1
        blocks.append(np.where(lower, (t >= (i // h) * h) & (t <= i), (t > i) & (t <= (i // h + 1) * h - 1)))
    sums = np.concatenate(blocks, axis=0).astype(np.float32)
    sums = np.concatenate([sums] * 3, axis=1)
    msb = np.floor(np.log2(np.maximum(i ^ t, 1))).astype(np.int32)
    lvl = np.where(t < i, _UNIT_LEVEL - msb, np.where(t == i, _DIAG_LEVEL, -1)).astype(np.int32)
    lvl_pair = np.full((PAIR * c, PAIR * c), -1, np.int32)
    for p in range(PAIR):
        lvl_pair[p * c:(p + 1) * c, p * c:(p + 1) * c] = lvl
    return sums, lvl_pair


def _gla_chunks(q_tiles, k_tiles, vs, g_tiles, sums_ref, lvl, st_ref):
    c = CHUNK
    n_tiles = len(q_tiles)
    per_tile = HEADS // n_tiles
    rows = q_tiles[0].shape[0]
    g_all = jnp.concatenate(g_tiles, axis=-1)
    g_wide = jnp.concatenate([g_all[p * c:(p + 1) * c] for p in range(PAIR)], axis=-1)
    e_all = _dot(sums_ref[...], _split3(g_wide))
    head_of_lane = lax.broadcasted_iota(jnp.int32, (rows, LANES), 1) // (LANES // per_tile)
    outs = [None] * HEADS
    for t, (q, k, g) in enumerate(zip(q_tiles, k_tiles, g_tiles)):
        e = [e_all[:, (p * n_tiles + t) * LANES:(p * n_tiles + t + 1) * LANES] for p in range(PAIR)]
        block = lambda b: jnp.concatenate([ep[b * c:(b + 1) * c] for ep in e], axis=0)
        qm = [q if per_tile == 1 else jnp.where(head_of_lane == j, q, 0.0) for j in range(per_tile)]
        attn = [jnp.where(lvl == _DIAG_LEVEL, jnp.sum(qj * k, axis=-1, keepdims=True), 0.0) for qj in qm]
        x = jnp.exp(g)
        kx = k.astype(BF16)
        attn = [jnp.where(lvl == _UNIT_LEVEL, _dot_nt((qj * x).astype(BF16), kx), a) for qj, a in zip(qm, attn)]
        for l in range(len(_LEVELS)):
            x = jnp.exp(block(2 + l))
            kx = (k * x).astype(BF16)
            attn = [jnp.where(lvl == l, _dot_nt((qj * x).astype(BF16), kx), a) for qj, a in zip(qm, attn)]
        x_in = jnp.exp(block(0))
        k_out = (k * jnp.exp(block(1))).astype(BF16)
        for j in range(per_tile):
            h = t * per_tile + j
            vb = vs[h].astype(BF16)
            o_intra = _dot(attn[j].astype(BF16), vb)
            q_in = (qm[j] * x_in).astype(BF16)
            st = st_ref[h]
            o_inter = []
            for p in range(PAIR):
                rs = slice(p * c, (p + 1) * c)
                o_inter.append(_dot_nt(q_in[rs], st.astype(BF16)))
                st = st * jnp.exp(e[p][c - 1:c]) + _dot_tn(vb[rs], k_out[rs])
            st_ref[h] = st
            outs[h] = o_intra + jnp.concatenate(o_inter, axis=0)
    return outs


def _tiles(x):
    return [x[:, t * LANES:(t + 1) * LANES] for t in range(x.shape[1] // LANES)]


def _heads(x):
    return [x[:, h * LANES:(h + 1) * LANES] for h in range(HEADS)]


def _store_heads(o_ref, sl, outs):
    for h, o in enumerate(outs):
        o_ref[sl, h * LANES:(h + 1) * LANES] = o.astype(o_ref.dtype)


def _gla_kernel(q_ref, k_ref, v_ref, r_ref, lr_ref, gw_ref, gb_ref, ng_ref, sums_ref, lvl_ref, o_ref, st_ref):
    @pl.when(pl.program_id(1) == 0)
    def _():
        st_ref[...] = jnp.zeros_like(st_ref)

    lvl = lvl_ref[...]
    ng = ng_ref[...]
    rows = PAIR * CHUNK

    def body(ci, carry):
        sl = pl.ds(pl.multiple_of(ci * rows, rows), rows)
        g = _log_sigmoid(_dot(lr_ref[sl, :].astype(BF16), gw_ref[...]) + gb_ref[...]) / GLA_GATE_NORM
        outs = _gla_chunks(_tiles(q_ref[sl, :] * GLA_DK ** -0.5), _tiles(k_ref[sl, :]), _heads(v_ref[sl, :]),
                           _tiles(g), sums_ref, lvl, st_ref)
        r = _heads(_silu(r_ref[sl, :]))
        _store_heads(o_ref, sl, [_rms_norm(o, ng) * rh for o, rh in zip(outs, r)])
        return carry

    lax.fori_loop(0, q_ref.shape[0] // rows, body, 0, unroll=CHUNK_UNROLL // PAIR)


def _hgrn_kernel(q_ref, f_ref, v_ref, r_ref, lb_ref, ng_ref, sums_ref, lvl_ref, o_ref, st_ref):
    @pl.when(pl.program_id(1) == 0)
    def _():
        st_ref[...] = jnp.zeros_like(st_ref)

    lvl = lvl_ref[...]
    ng = ng_ref[...]
    lb = lb_ref[...]
    lb_floor = jnp.maximum(lb, LB_EPS)
    rows = PAIR * CHUNK

    def body(ci, carry):
        sl = pl.ds(pl.multiple_of(ci * rows, rows), rows)
        f_pre = f_ref[sl, :]
        t = jnp.exp(-jnp.abs(f_pre))
        r = 1.0 / (1.0 + t)
        sig_pos = jnp.where(f_pre >= 0, r, t * r)
        sig_neg = jnp.where(f_pre >= 0, t * r, r)
        g = jnp.log(lb_floor + (1.0 - lb) * sig_pos)
        k = (1.0 - lb) * sig_neg
        q = _silu(q_ref[sl, :]) * HGRN_DK ** -0.5
        outs = _gla_chunks(_tiles(q), _tiles(k), _heads(v_ref[sl, :]), _tiles(g), sums_ref, lvl, st_ref)
        r = _heads(_silu(r_ref[sl, :]))
        _store_heads(o_ref, sl, [_rms_norm(o, ng) * rh for o, rh in zip(outs, r)])
        return carry

    lax.fori_loop(0, q_ref.shape[0] // rows, body, 0, unroll=CHUNK_UNROLL // PAIR)


def _group_block(tt, nt, idx):
    return pl.BlockSpec((tt, GW), lambda b, t: (b * nt + t, idx))


def _const_block(shape):
    return pl.BlockSpec(shape, lambda *_: tuple(0 for _ in shape))


def _packed_specs(tt, nt):
    qkw = GW // QK_PACK
    row = lambda b, t: b * nt + t
    return [pl.BlockSpec((tt, qkw), lambda b, t: (row(b, t), 0)),
            pl.BlockSpec((tt, qkw), lambda b, t: (row(b, t), 1)),
            pl.BlockSpec((tt, GW), lambda b, t: (row(b, t), (2 * qkw) // GW)),
            pl.BlockSpec((tt, GW), lambda b, t: (row(b, t), (2 * qkw) // GW + 1)),
            pl.BlockSpec((tt, LANES), lambda b, t: (row(b, t), (2 * qkw + 2 * GW) // LANES))]


def gla_mixer(proj, gate_w, gate_b, norm_g, *, batch, tt=512):
    t = proj.shape[0]
    s = t // batch
    tt = min(tt, s)
    nt = s // tt
    sums, lvl = _gla_tables()
    gb = functools.partial(_group_block, tt, nt)
    return pl.pallas_call(
        _gla_kernel,
        grid=(batch, nt),
        in_specs=_packed_specs(tt, nt)
        + [_const_block(gate_w.shape), _const_block(gate_b.shape),
           _const_block((1, LANES)), _const_block(sums.shape), _const_block(lvl.shape)],
        out_specs=gb(0),
        out_shape=jax.ShapeDtypeStruct((t, GW), BF16),
        scratch_shapes=[pltpu.VMEM((HEADS, LANES, LANES), F32)],
        compiler_params=_cparams(("parallel", "arbitrary")),
        name="gla",
    )(proj, proj, proj, proj, proj, gate_w, gate_b, norm_g.reshape(1, LANES),
      jnp.asarray(sums, BF16), jnp.asarray(lvl))


def hgrn_mixer(proj, lower_bound, norm_g, *, batch, tt=512):
    t = proj.shape[0]
    s = t // batch
    tt = min(tt, s)
    nt = s // tt
    sums, lvl = _gla_tables()
    gb = functools.partial(_group_block, tt, nt)
    return pl.pallas_call(
        _hgrn_kernel,
        grid=(batch, nt),
        in_specs=[gb(0), gb(1), gb(2), gb(3),
                  _const_block((1, GW)),
                  _const_block((1, LANES)), _const_block(sums.shape), _const_block(lvl.shape)],
        out_specs=gb(0),
        out_shape=jax.ShapeDtypeStruct((t, GW), BF16),
        scratch_shapes=[pltpu.VMEM((HEADS, LANES, LANES), F32)],
        compiler_params=_cparams(("parallel", "arbitrary")),
        name="hgrn",
    )(proj, proj, proj, proj, lower_bound.reshape(1, GW), norm_g.reshape(1, LANES),
      jnp.asarray(sums, BF16), jnp.asarray(lvl))


def _mlstm_kernel(q_ref, k_ref, v_ref, og_ref, if_ref, cw_q_ref, cw_k_ref, gb_ref, o_ref,
                  st_ref, m_ref, qx_ref, kx_ref):
    tt = q_ref.shape[0]

    @pl.when(pl.program_id(1) == 0)
    def _():
        st_ref[...] = jnp.zeros_like(st_ref)
        m_ref[...] = jnp.zeros_like(m_ref)
        qx_ref[0:CONV_PAD, :] = jnp.zeros((CONV_PAD, qx_ref.shape[1]), F32)
        kx_ref[0:CONV_PAD, :] = jnp.zeros((CONV_PAD, kx_ref.shape[1]), F32)

    qx_ref[CONV_PAD:CONV_PAD + tt, :] = q_ref[...]
    kx_ref[CONV_PAD:CONV_PAD + tt, :] = k_ref[...]

    c = CHUNK
    rows = PAIR * c

    def conv_silu(ext, cw, r0):
        xw = ext[pl.ds(r0, rows + CONV_PAD), :]
        acc = xw[CONV_PAD - (CONV_WIDTH - 1):CONV_PAD - (CONV_WIDTH - 1) + rows] * cw[0:1, :]
        for w in range(1, CONV_WIDTH):
            acc = acc + xw[CONV_PAD - (CONV_WIDTH - 1) + w:CONV_PAD - (CONV_WIDTH - 1) + w + rows] * cw[w:w + 1, :]
        return _silu(acc)

    row = lax.broadcasted_iota(jnp.int32, (rows, rows), 0)
    col = lax.broadcasted_iota(jnp.int32, (rows, rows), 1)
    causal = (col <= row) & (col >= (row // c) * c)
    eye = col == row
    tril3 = jnp.concatenate([jnp.where(causal, 1.0, 0.0).astype(BF16)] * 3, axis=1)
    chunk_of_row = lax.broadcasted_iota(jnp.int32, (rows, 1), 0) // c
    head_of_lane = lax.broadcasted_iota(jnp.int32, (rows, LANES), 1) // MLSTM_DK
    ones_col = jnp.where(lax.broadcasted_iota(jnp.int32, (rows, LANES), 1) == 0, 1.0, 0.0)
    gbias = gb_ref[...]

    def body(pi, carry):
        r0 = pl.multiple_of(pi * rows, rows)
        sl = pl.ds(r0, rows)
        gates = if_ref[sl, :] + gbias
        cum_all = _dot(tril3, _split3(_log_sigmoid(gates)))
        qc = _tiles(conv_silu(qx_ref, cw_q_ref, r0))
        kc = _tiles(conv_silu(kx_ref, cw_k_ref, r0) * MLSTM_DK ** -0.5)
        vs = _heads(v_ref[sl, :])
        og = _heads(_sigmoid(og_ref[sl, :]))
        outs = []
        for h in range(HEADS):
            i_col = gates[:, h:h + 1]
            cum = cum_all[:, HEADS + h:HEADS + h + 1]
            a_col = i_col - cum
            a_row = jnp.sum(jnp.where(eye, a_col, 0.0), axis=0, keepdims=True)
            m_in = [m_ref[h]]
            cum_last, lli = [], []
            for p in range(PAIR):
                cum_last.append(cum[(p + 1) * c - 1:(p + 1) * c, :])
                lli.append(cum_last[p] + a_col[p * c:(p + 1) * c])
                m_in.append(jnp.maximum(cum_last[p] + m_in[p], jnp.max(lli[p], axis=0, keepdims=True)))
            m_st = m_in[0]
            for p in range(1, PAIR):
                m_st = jnp.where(chunk_of_row >= p, m_in[p], m_st)
            log_intra = jnp.where(causal, cum + a_row, MASK_NEG)
            log_inter = cum + m_st
            m_t = jnp.maximum(log_inter, jnp.max(log_intra, axis=-1, keepdims=True))
            w_inter = jnp.exp(log_inter - m_t)
            w_intra = jnp.where(causal, jnp.exp(log_intra - m_t), 0.0)
            qb = jnp.where(head_of_lane == h % QK_PACK, qc[h // QK_PACK], 0.0).astype(BF16)
            kf = kc[h // QK_PACK]
            v_ext = jnp.concatenate([vs[h], ones_col], axis=-1).astype(BF16)
            scores = _dot_nt(qb, kf.astype(BF16)) * w_intra
            nd_intra = _dot(scores.astype(BF16), v_ext)
            st = st_ref[h]
            nd_inter = []
            for p in range(PAIR):
                rs = slice(p * c, (p + 1) * c)
                nd_inter.append(_dot(qb[rs], st.astype(BF16)))
                wk = jnp.exp(lli[p] - m_in[p + 1])
                st = (jnp.exp(cum_last[p] + m_in[p] - m_in[p + 1]) * st
                      + _dot_tn((kf[rs] * wk).astype(BF16), v_ext[rs]))
            st_ref[h] = st
            m_ref[h] = m_in[PAIR]
            nd = w_inter * jnp.concatenate(nd_inter, axis=0) + nd_intra
            den = jnp.maximum(jnp.abs(nd[:, LANES:LANES + 1]), jnp.exp(-m_t))
            outs.append(nd[:, :LANES] / den * og[h])
        _store_heads(o_ref, sl, outs)
        return carry

    lax.fori_loop(0, tt // rows, body, 0, unroll=CHUNK_UNROLL // PAIR)
    qx_ref[0:CONV_PAD, :] = qx_ref[tt:tt + CONV_PAD, :]
    kx_ref[0:CONV_PAD, :] = kx_ref[tt:tt + CONV_PAD, :]


def mlstm_mixer(proj, conv_w_q, conv_w_k, gate_b, *, batch, tt=512):
    t = proj.shape[0]
    s = t // batch
    tt = min(tt, s)
    nt = s // tt
    gb = functools.partial(_group_block, tt, nt)
    return pl.pallas_call(
        _mlstm_kernel,
        grid=(batch, nt),
        in_specs=_packed_specs(tt, nt)
        + [_const_block(conv_w_q.shape), _const_block(conv_w_k.shape), _const_block((1, LANES))],
        out_specs=gb(0),
        out_shape=jax.ShapeDtypeStruct((t, GW), BF16),
        scratch_shapes=[pltpu.VMEM((HEADS, LANES, 2 * LANES), F32), pltpu.VMEM((HEADS, 1, 1), F32),
                        pltpu.VMEM((CONV_PAD + tt, GW // QK_PACK), F32),
                        pltpu.VMEM((CONV_PAD + tt, GW // QK_PACK), F32)],
        compiler_params=_cparams(("parallel", "arbitrary")),
        name="mlstm",
    )(proj, proj, proj, proj, proj, conv_w_q, conv_w_k, gate_b)


LOG2E = math.log2(math.e)


def _diff_attn_kernel(sc_ref, q_ref, k_ref, vt_ref, bias_ref, ng_ref, o_ref,
                      s_ref, p_ref, mx_ref, m_ref, c_ref, a_ref):
    qi = pl.program_id(2)
    q = q_ref[...].astype(F32) * (DIFF_DK ** -0.5 * LOG2E)
    lane = lax.broadcasted_iota(jnp.int32, q.shape, 1)
    qs = (jnp.where(lane < DIFF_DK, q, 0.0).astype(BF16), jnp.where(lane >= DIFF_DK, q, 0.0).astype(BF16))
    m_ref[...] = jnp.full(m_ref.shape, MASK_NEG, F32)
    c_ref[...] = jnp.ones_like(c_ref)
    a_ref[...] = jnp.zeros_like(a_ref)
    p_ref[1] = jnp.zeros(p_ref.shape[1:], BF16)
    tk = vt_ref.shape[2]
    ones_rows = jnp.where(lax.broadcasted_iota(jnp.int32, (BF16_SUBLANES, tk), 0) == 0, 1.0, 0.0).astype(BF16)

    def scores(kt, slot):
        kk = k_ref[pl.ds(pl.multiple_of(kt * tk, tk), tk), :]
        tile = jnp.minimum(qi - kt, 2)
        for mp in range(2):
            s = _dot_nt(kk, qs[mp]) + bias_ref[tile]
            s_ref[slot, mp] = s
            mx_ref[slot, mp] = jnp.max(s, axis=0, keepdims=True)

    def values(kt, slot):
        vt = jnp.concatenate([vt_ref[kt], ones_rows], axis=0)
        for mp in range(2):
            a_ref[mp] = a_ref[mp] * c_ref[mp] + _dot(vt, p_ref[slot, mp])

    scores(0, 0)

    def step(ki, cur):
        nxt = 1 - cur
        scores(jnp.minimum(ki + 1, qi), nxt)
        values(jnp.maximum(ki - 1, 0), nxt)
        for mp in range(2):
            m_old = m_ref[mp]
            m_new = jnp.maximum(m_old, mx_ref[cur, mp])
            p_ref[cur, mp] = jnp.exp2(s_ref[cur, mp] - m_new).astype(BF16)
            c_ref[mp] = jnp.exp2(m_old - m_new)
            m_ref[mp] = m_new

    def pair(j, carry):
        step(2 * j, 0)
        step(2 * j + 1, 1)
        return carry

    n_tiles = qi + 1
    lax.fori_loop(0, n_tiles // 2, pair, 0)

    @pl.when(lax.rem(n_tiles, 2) == 1)
    def _():
        step(qi, 0)

    values(qi, lax.rem(qi, 2))
    lam = sc_ref[0]
    dv = o_ref.shape[1]
    out_t = (a_ref[0, 0:dv] / a_ref[0, dv:dv + 1] - lam * (a_ref[1, 0:dv] / a_ref[1, dv:dv + 1]))
    out_t = out_t * lax.rsqrt(jnp.mean(out_t * out_t, axis=0, keepdims=True) + LN_EPS)
    o_ref[...] = (out_t.T * ng_ref[...] * sc_ref[1]).astype(o_ref.dtype)


def diff_attn_mixer(qk, vt, bias_tiles, scalars, norm_g, *, batch, tq):
    t = qk.shape[0]
    s = t // batch
    nt = s // tq
    dve = LANES + BF16_SUBLANES
    return pl.pallas_call(
        _diff_attn_kernel,
        grid=(batch, HEADS, nt),
        in_specs=[pl.BlockSpec(memory_space=pltpu.SMEM),
                  pl.BlockSpec((tq, LANES), lambda b, h, t: (b * nt + t, h)),
                  pl.BlockSpec((s, LANES), lambda b, h, t: (b, HEADS + h)),
                  pl.BlockSpec((None, nt, LANES, tq), lambda b, h, t: (b * HEADS + h, 0, 0, 0)),
                  pl.BlockSpec((None, 3, tq, tq), lambda b, h, t: (h, 0, 0, 0)),
                  _const_block((1, LANES))],
        out_specs=pl.BlockSpec((tq, LANES), lambda b, h, t: (b * nt + t, h)),
        out_shape=jax.ShapeDtypeStruct((t, HEADS * LANES), BF16),
        scratch_shapes=[pltpu.VMEM((2, 2, tq, tq), F32), pltpu.VMEM((2, 2, tq, tq), BF16),
                        pltpu.VMEM((2, 2, 1, tq), F32), pltpu.VMEM((2, 1, tq), F32), pltpu.VMEM((2, 1, tq), F32),
                        pltpu.VMEM((2, dve, tq), F32)],
        compiler_params=_cparams(("parallel", "parallel", "arbitrary")),
        name="diff_attn",
    )(scalars, qk, qk, vt, bias_tiles, norm_g.reshape(1, LANES))


def _t5_bucket_map(rel):
    n = np.maximum(rel, 0)
    max_exact = T5_BUCKETS // 2
    large = max_exact + (np.log(np.maximum(n, 1).astype(np.float32) / np.float32(max_exact))
                         / np.float32(math.log(T5_MAX_DIST / max_exact)) * (T5_BUCKETS - max_exact)).astype(np.int32)
    large = np.clip(large, max_exact, T5_BUCKETS - 1)
    return np.where(n < max_exact, n, large).astype(np.int32)


def t5_bias_tiles(t5_table, tq):
    assert tq >= T5_MAX_DIST
    j = np.arange(tq)[:, None]
    i = np.arange(tq)[None, :]
    rel = np.stack([d * tq + i - j for d in range(3)])
    bucket = jnp.asarray(_t5_bucket_map(rel).astype(np.int8))
    table = t5_table.astype(F32)
    tiles = jnp.zeros((HEADS,) + rel.shape, F32)
    for b in range(T5_BUCKETS):
        tiles = jnp.where(bucket[None] == b, table[b][:, None, None, None], tiles)
    return jnp.where(jnp.asarray(rel >= 0)[None], tiles * LOG2E, MASK_NEG)


def _xattn_kernel(xb_ref, wq_ref, k_ref, v_ref, o_ref):
    d = xb_ref.shape[1]
    hd = d // N_XHEADS
    q = _dot(xb_ref[...], wq_ref[...]).astype(BF16)
    for h in range(N_XHEADS):
        hs = slice(h * hd, (h + 1) * hd)
        s = _dot_nt(q[:, hs], k_ref[:, hs]) * hd ** -0.5
        s = s - jnp.max(s, axis=-1, keepdims=True)
        p = jnp.exp(s)
        p = p / jnp.sum(p, axis=-1, keepdims=True)
        o_ref[:, hs] = _dot(p.astype(BF16), v_ref[:, hs]).astype(o_ref.dtype)


def xattn(xb, wq, kv, *, batch, tm=512):
    t, d = xb.shape
    s = t // batch
    tm = min(tm, s)
    nt = s // tm
    n_mem = kv.shape[0] // batch
    return pl.pallas_call(
        _xattn_kernel,
        grid=(t // tm,),
        in_specs=[pl.BlockSpec((tm, d), lambda i: (i, 0)),
                  pl.BlockSpec((d, d), lambda i: (0, 0)),
                  pl.BlockSpec((n_mem, d), lambda i: (i // nt, 0)),
                  pl.BlockSpec((n_mem, d), lambda i: (i // nt, 1))],
        out_specs=pl.BlockSpec((tm, d), lambda i: (i, 0)),
        out_shape=jax.ShapeDtypeStruct((t, d), BF16),
        compiler_params=_cparams(("parallel",)),
        name="xattn",
    )(xb, wq, kv, kv)


def _pad_cols(w, n):
    return jnp.pad(w, [(0, 0)] * (w.ndim - 1) + [(0, n - w.shape[-1])])


def _split_in_proj(w_in):
    gw = HEADS * LANES
    sizes = (HEADS * GLA_DK, HEADS * GLA_DK, gw, GLA_GATE_RANK, gw,
             HEADS * HGRN_DK, HEADS * HGRN_DK, gw, gw,
             2 * HEADS * DIFF_DK, 2 * HEADS * DIFF_DK, gw,
             HEADS * MLSTM_DK, HEADS * MLSTM_DK, gw, 2 * HEADS, gw)
    assert sum(sizes) == w_in.shape[-1]
    (a_q, a_k, a_v, a_lr, a_r, b_q, b_f, b_i, b_g, c_q, c_k, c_v,
     d_q, d_k, d_v, d_if, d_o) = jnp.split(w_in, list(np.cumsum(sizes)[:-1]), axis=-1)
    w_gla = jnp.concatenate([a_q, a_k, a_v, a_r, _pad_cols(a_lr, LANES)], -1)
    w_hgrn = jnp.concatenate([b_q, b_f, b_i, b_g], -1)
    w_diff = jnp.concatenate([c_q, c_k, c_v], -1)
    w_mlstm = jnp.concatenate([d_q, d_k, d_v, d_o, _pad_cols(d_if, LANES)], -1)
    return [w.astype(BF16) for w in (w_gla, w_hgrn, w_diff, w_mlstm)]


def kernel(x, mem, ln_g, ln_b, ffn_w_in, ffn_w_out, w_in, w_out, gla_gate_w, gla_gate_b, gla_norm_g, hgrn_lb, hgrn_norm_g, diff_lambda, diff_norm_g, t5_table, mlstm_conv_w, mlstm_gate_b, xattn_w_q, xattn_w_kv, xattn_w_o):
    batch, seq, d = x.shape
    depth = ln_g.shape[0]
    alpha = (2 * depth) ** 0.25
    t = batch * seq

    sm = jax.nn.softmax(hgrn_lb.astype(F32), axis=0)
    lower_bounds = jnp.clip(jnp.cumsum(sm, axis=0) - sm[0], 0.0, 1.0 - 1e-6)
    tq = min(512, seq)
    bias_tiles = t5_bias_tiles(t5_table, tq)

    x = x.reshape(t, d)
    xb = x.astype(BF16)
    memb = mem.reshape(-1, d).astype(BF16)
    for l in range(depth):
        x, xb = ffn_ln(x, xb, ffn_w_in[l, 0].astype(BF16), ffn_w_out[l, 0].astype(BF16), ln_g[l, 0], ln_b[l, 0], alpha=alpha)

        w_gla, w_hgrn, w_diff, w_mlstm = _split_in_proj(w_in[l])
        o_a = gla_mixer(matmul(xb, w_gla), _pad_cols(gla_gate_w[l].T, LANES).T.astype(BF16),
                        gla_gate_b[l].reshape(1, -1), gla_norm_g[l], batch=batch)
        o_b = hgrn_mixer(matmul(xb, w_hgrn), lower_bounds[l], hgrn_norm_g[l], batch=batch)
        lambda_init = 0.8 - 0.6 * math.exp(-0.3 * l)
        lq1, lk1, lq2, lk2 = [diff_lambda[l, j].astype(F32) for j in range(4)]
        lam = jnp.exp(jnp.sum(lq1 * lk1)) - jnp.exp(jnp.sum(lq2 * lk2)) + lambda_init
        p_c = matmul(xb, w_diff, out_dtype=BF16)
        vt_c = (p_c[:, 2 * GW:].reshape(batch, seq // tq, tq, HEADS, LANES).transpose(0, 3, 1, 4, 2)
                .reshape(batch * HEADS, seq // tq, LANES, tq))
        o_c = diff_attn_mixer(p_c, vt_c, bias_tiles,
                              jnp.stack([lam, jnp.asarray(1.0 - lambda_init, F32)]).astype(F32), diff_norm_g[l], batch=batch, tq=tq)
        cw = mlstm_conv_w[l]
        o_d = mlstm_mixer(matmul(xb, w_mlstm), cw[:, :HEADS * MLSTM_DK], cw[:, HEADS * MLSTM_DK:],
                          _pad_cols(mlstm_gate_b[l].reshape(1, -1), LANES), batch=batch)
        x, xb = proj_ln(x, [o_a, o_b, o_c, o_d], w_out[l].astype(BF16), ln_g[l, 1], ln_b[l, 1], alpha=alpha)

        kv = matmul(memb, xattn_w_kv[l].astype(BF16), out_dtype=BF16, tn=1024)
        att = xattn(xb, xattn_w_q[l].astype(BF16), kv, batch=batch)
        x, xb = proj_ln(x, [att], xattn_w_o[l].astype(BF16), ln_g[l, 2], ln_b[l, 2], alpha=alpha)

        x, xb = ffn_ln(x, xb, ffn_w_in[l, 1].astype(BF16), ffn_w_out[l, 1].astype(BF16), ln_g[l, 3], ln_b[l, 3], alpha=alpha)
    return x.reshape(batch, seq, d)
```

```python
import functools
import math

import numpy as np
import jax
import jax.numpy as jnp
from jax import lax
from jax.experimental import pallas as pl
from jax.experimental.pallas import tpu as pltpu

F32 = jnp.float32
BF16 = jnp.bfloat16

HEADS = 4
N_XHEADS = 4
LANES = 128
BF16_SUBLANES = 16
GLA_DK = 64
HGRN_DK = 128
DIFF_DK = 64
MLSTM_DK = 64
GLA_GATE_RANK = 16
GLA_GATE_NORM = 16.0
CONV_WIDTH = 4
CHUNK = 64
T5_BUCKETS = 32
T5_MAX_DIST = 128
LN_EPS = 1e-5
LB_EPS = 1e-12
MASK_NEG = -1e30
CONV_PAD = 8
CHUNK_UNROLL = 8
PAIR = 2
QK_PACK = LANES // GLA_DK

VMEM_LIMIT = 56 * 1024 * 1024


def _cparams(sem):
    return pltpu.CompilerParams(dimension_semantics=sem, vmem_limit_bytes=VMEM_LIMIT)


def _sigmoid(x):
    return 1.0 / (1.0 + jnp.exp(-x))


def _silu(x):
    return x * _sigmoid(x)


def _log_sigmoid(x):
    return jnp.minimum(x, 0.0) - jnp.log1p(jnp.exp(-jnp.abs(x)))


def _layer_norm(y, g, b):
    mu = jnp.mean(y, axis=-1, keepdims=True)
    d = y - mu
    var = jnp.mean(d * d, axis=-1, keepdims=True)
    return d * lax.rsqrt(var + LN_EPS) * g + b


def _rms_norm(y, g):
    return y * lax.rsqrt(jnp.mean(y * y, axis=-1, keepdims=True) + LN_EPS) * g


def _dot(a, b):
    return jnp.dot(a, b, preferred_element_type=F32)


def _dot_nt(a, b):
    return lax.dot_general(a, b, (((1,), (1,)), ((), ())), preferred_element_type=F32)


def _dot_tn(a, b):
    return lax.dot_general(a, b, (((0,), (0,)), ((), ())), preferred_element_type=F32)


def _split3(g):
    hi = g.astype(BF16)
    r1 = g - hi.astype(F32)
    mid = r1.astype(BF16)
    lo = (r1 - mid.astype(F32)).astype(BF16)
    return jnp.concatenate([hi, mid, lo], axis=0)


ROW_SPLIT = 2


def _ffn_ln_kernel(x_ref, xb_ref, wg_ref, wu_ref, wo_ref, g_ref, b_ref, o_ref, ob_ref, acc_ref, *, alpha, nf):
    j = pl.program_id(1)
    tm = x_ref.shape[0]

    def hidden():
        xb = xb_ref[...]
        return (_silu(_dot(xb, wg_ref[...])) * _dot(xb, wu_ref[...])).astype(BF16)

    def finish(first):
        h = hidden()
        rows = tm // ROW_SPLIT
        for r in range(ROW_SPLIT):
            rs = slice(r * rows, (r + 1) * rows)
            y = _dot(h[rs], wo_ref[...])
            if not first:
                y = y + acc_ref[rs, :]
            y = _layer_norm(alpha * x_ref[rs, :] + 0.5 * y, g_ref[...], b_ref[...])
            o_ref[rs, :] = y
            ob_ref[rs, :] = y.astype(BF16)

    if nf == 1:
        finish(True)
        return

    @pl.when(j == 0)
    def _():
        acc_ref[...] = _dot(hidden(), wo_ref[...])

    @pl.when((j > 0) & (j < nf - 1))
    def _():
        acc_ref[...] += _dot(hidden(), wo_ref[...])

    @pl.when(j == nf - 1)
    def _():
        finish(False)


def ffn_ln(x, xb, w_in, w_out, g, b, *, alpha, tm=512, tf=512):
    t, d = x.shape
    f = w_out.shape[0]
    tm, tf = min(tm, t), min(tf, f)
    nf = f // tf
    return pl.pallas_call(
        functools.partial(_ffn_ln_kernel, alpha=alpha, nf=nf),
        grid=(t // tm, nf),
        in_specs=[
            pl.BlockSpec((tm, d), lambda i, j: (i, 0)),
            pl.BlockSpec((tm, d), lambda i, j: (i, 0)),
            pl.BlockSpec((d, tf), lambda i, j: (0, j)),
            pl.BlockSpec((d, tf), lambda i, j: (0, j + nf)),
            pl.BlockSpec((tf, d), lambda i, j: (j, 0)),
            pl.BlockSpec((1, d), lambda i, j: (0, 0)),
            pl.BlockSpec((1, d), lambda i, j: (0, 0)),
        ],
        out_specs=[pl.BlockSpec((tm, d), lambda i, j: (i, 0)), pl.BlockSpec((tm, d), lambda i, j: (i, 0))],
        out_shape=[jax.ShapeDtypeStruct((t, d), F32), jax.ShapeDtypeStruct((t, d), BF16)],
        scratch_shapes=[pltpu.VMEM((tm, d), F32)],
        compiler_params=_cparams(("parallel", "arbitrary")),
        name="ffn_ln",
    )(x, xb, w_in, w_in, w_out, g.reshape(1, d), b.reshape(1, d))


def _matmul_kernel(x_ref, w_ref, o_ref):
    o_ref[...] = _dot(x_ref[...], w_ref[...]).astype(o_ref.dtype)


def matmul(xb, w, *, out_dtype=F32, tm=1024, tn=None):
    t, k = xb.shape
    n = w.shape[1]
    tm = min(tm, t)
    tn = n if tn is None else min(tn, n)
    return pl.pallas_call(
        _matmul_kernel,
        grid=(n // tn, t // tm),
        in_specs=[pl.BlockSpec((tm, k), lambda j, i: (i, 0)), pl.BlockSpec((k, tn), lambda j, i: (0, j))],
        out_specs=pl.BlockSpec((tm, tn), lambda j, i: (i, j)),
        out_shape=jax.ShapeDtypeStruct((t, n), out_dtype),
        compiler_params=_cparams(("parallel", "parallel")),
        name="matmul",
    )(xb, w)


def _proj_ln_kernel(*refs, alpha, n_parts):
    x_ref = refs[0]
    parts = refs[1:1 + n_parts]
    w_ref, g_ref, b_ref, o_ref, ob_ref = refs[1 + n_parts:]
    kp = parts[0].shape[1]
    rows = x_ref.shape[0] // ROW_SPLIT
    for r in range(ROW_SPLIT):
        rs = slice(r * rows, (r + 1) * rows)
        acc = alpha * x_ref[rs, :]
        for p in range(n_parts):
            acc = acc + _dot(parts[p][rs, :], w_ref[p * kp:(p + 1) * kp, :])
        y = _layer_norm(acc, g_ref[...], b_ref[...])
        o_ref[rs, :] = y
        ob_ref[rs, :] = y.astype(BF16)


def proj_ln(x, parts, w, g, b, *, alpha, tm=512):
    t, d = x.shape
    tm = min(tm, t)
    n_parts = len(parts)
    kp = parts[0].shape[1]
    return pl.pallas_call(
        functools.partial(_proj_ln_kernel, alpha=alpha, n_parts=n_parts),
        grid=(t // tm,),
        in_specs=[pl.BlockSpec((tm, d), lambda i: (i, 0))]
        + [pl.BlockSpec((tm, kp), lambda i: (i, 0)) for _ in parts]
        + [pl.BlockSpec((n_parts * kp, d), lambda i: (0, 0)),
           pl.BlockSpec((1, d), lambda i: (0, 0)), pl.BlockSpec((1, d), lambda i: (0, 0))],
        out_specs=[pl.BlockSpec((tm, d), lambda i: (i, 0)), pl.BlockSpec((tm, d), lambda i: (i, 0))],
        out_shape=[jax.ShapeDtypeStruct((t, d), F32), jax.ShapeDtypeStruct((t, d), BF16)],
        compiler_params=_cparams(("parallel",)),
        name="proj_ln",
    )(x, *parts, w, g.reshape(1, d), b.reshape(1, d))


_LEVELS = tuple(CHUNK >> (s + 1) for s in range(int(math.log2(CHUNK)) - 1))
_UNIT_LEVEL = len(_LEVELS)
_DIAG_LEVEL = _UNIT_LEVEL + 1
GW = HEADS * LANES


def _gla_tables():
    c = CHUNK
    i = np.arange(c)[:, None]
    t = np.arange(c)[None, :]
    blocks = [t <= i, t > i]
    for h in _LEVELS:
        lower = (i // h) % 2 == 1
        blocks.append(np.where(lower, (t >= (i // h) * h) & (t <= i), (t > i) & (t <= (i // h + 1) * h - 1)))
    sums = np.concatenate(blocks, axis=0).astype(np.float32)
    sums = np.concatenate([sums] * 3, axis=1)
    msb = np.floor(np.log2(np.maximum(i ^ t, 1))).astype(np.int32)
    lvl = np.where(t < i, _UNIT_LEVEL - msb, np.where(t == i, _DIAG_LEVEL, -1)).astype(np.int32)
    lvl_pair = np.full((PAIR * c, PAIR * c), -1, np.int32)
    for p in range(PAIR):
        lvl_pair[p * c:(p + 1) * c, p * c:(p + 1) * c] = lvl
    return sums, lvl_pair


def _gla_chunks(q_tiles, k_tiles, vs, g_tiles, sums_ref, lvl, st_ref):
    c = CHUNK
    n_tiles = len(q_tiles)
    per_tile = HEADS // n_tiles
    rows = q_tiles[0].shape[0]
    g_all = jnp.concatenate(g_tiles, axis=-1)
    g_wide = jnp.concatenate([g_all[p * c:(p + 1) * c] for p in range(PAIR)], axis=-1)
    e_all = _dot(sums_ref[...], _split3(g_wide))
    head_of_lane = lax.broadcasted_iota(jnp.int32, (rows, LANES), 1) // (LANES // per_tile)
    outs = [None] * HEADS
    for t, (q, k, g) in enumerate(zip(q_tiles, k_tiles, g_tiles)):
        e = [e_all[:, (p * n_tiles + t) * LANES:(p * n_tiles + t + 1) * LANES] for p in range(PAIR)]
        block = lambda b: jnp.concatenate([ep[b * c:(b + 1) * c] for ep in e], axis=0)
        qm = [q if per_tile == 1 else jnp.where(head_of_lane == j, q, 0.0) for j in range(per_tile)]
        attn = [jnp.where(lvl == _DIAG_LEVEL, jnp.sum(qj * k, axis=-1, keepdims=True), 0.0) for qj in qm]
        x = jnp.exp(g)
        kx = k.astype(BF16)
        attn = [jnp.where(lvl == _UNIT_LEVEL, _dot_nt((qj * x).astype(BF16), kx), a) for qj, a in zip(qm, attn)]
        for l in range(len(_LEVELS)):
            x = jnp.exp(block(2 + l))
            kx = (k * x).astype(BF16)
            attn = [jnp.where(lvl == l, _dot_nt((qj * x).astype(BF16), kx), a) for qj, a in zip(qm, attn)]
        x_in = jnp.exp(block(0))
        k_out = (k * jnp.exp(block(1))).astype(BF16)
        for j in range(per_tile):
            h = t * per_tile + j
            vb = vs[h].astype(BF16)
            o_intra = _dot(attn[j].astype(BF16), vb)
            q_in = (qm[j] * x_in).astype(BF16)
            st = st_ref[h]
            o_inter = []
            for p in range(PAIR):
                rs = slice(p * c, (p + 1) * c)
                o_inter.append(_dot_nt(q_in[rs], st.astype(BF16)))
                st = st * jnp.exp(e[p][c - 1:c]) + _dot_tn(vb[rs], k_out[rs])
            st_ref[h] = st
            outs[h] = o_intra + jnp.concatenate(o_inter, axis=0)
    return outs


def _tiles(x):
    return [x[:, t * LANES:(t + 1) * LANES] for t in range(x.shape[1] // LANES)]


def _heads(x):
    return [x[:, h * LANES:(h + 1) * LANES] for h in range(HEADS)]


def _store_heads(o_ref, sl, outs):
    for h, o in enumerate(outs):
        o_ref[sl, h * LANES:(h + 1) * LANES] = o.astype(o_ref.dtype)


def _gla_kernel(q_ref, k_ref, v_ref, r_ref, lr_ref, gw_ref, gb_ref, ng_ref, sums_ref, lvl_ref, o_ref, st_ref):
    @pl.when(pl.program_id(1) == 0)
    def _():
        st_ref[...] = jnp.zeros_like(st_ref)

    lvl = lvl_ref[...]
    ng = ng_ref[...]
    rows = PAIR * CHUNK

    def body(ci, carry):
        sl = pl.ds(pl.multiple_of(ci * rows, rows), rows)
        g = _log_sigmoid(_dot(lr_ref[sl, :].astype(BF16), gw_ref[...]) + gb_ref[...]) / GLA_GATE_NORM
        outs = _gla_chunks(_tiles(q_ref[sl, :] * GLA_DK ** -0.5), _tiles(k_ref[sl, :]), _heads(v_ref[sl, :]),
                           _tiles(g), sums_ref, lvl, st_ref)
        r = _heads(_silu(r_ref[sl, :]))
        _store_heads(o_ref, sl, [_rms_norm(o, ng) * rh for o, rh in zip(outs, r)])
        return carry

    lax.fori_loop(0, q_ref.shape[0] // rows, body, 0, unroll=CHUNK_UNROLL // PAIR)


def _hgrn_kernel(q_ref, f_ref, v_ref, r_ref, lb_ref, ng_ref, sums_ref, lvl_ref, o_ref, st_ref):
    @pl.when(pl.program_id(1) == 0)
    def _():
        st_ref[...] = jnp.zeros_like(st_ref)

    lvl = lvl_ref[...]
    ng = ng_ref[...]
    lb = lb_ref[...]
    lb_floor = jnp.maximum(lb, LB_EPS)
    rows = PAIR * CHUNK

    def body(ci, carry):
        sl = pl.ds(pl.multiple_of(ci * rows, rows), rows)
        f_pre = f_ref[sl, :]
        t = jnp.exp(-jnp.abs(f_pre))
        r = 1.0 / (1.0 + t)
        sig_pos = jnp.where(f_pre >= 0, r, t * r)
        sig_neg = jnp.where(f_pre >= 0, t * r, r)
        g = jnp.log(lb_floor + (1.0 - lb) * sig_pos)
        k = (1.0 - lb) * sig_neg
        q = _silu(q_ref[sl, :]) * HGRN_DK ** -0.5
        outs = _gla_chunks(_tiles(q), _tiles(k), _heads(v_ref[sl, :]), _tiles(g), sums_ref, lvl, st_ref)
        r = _heads(_silu(r_ref[sl, :]))
        _store_heads(o_ref, sl, [_rms_norm(o, ng) * rh for o, rh in zip(outs, r)])
        return carry

    lax.fori_loop(0, q_ref.shape[0] // rows, body, 0, unroll=CHUNK_UNROLL // PAIR)


def _group_block(tt, nt, idx):
    return pl.BlockSpec((tt, GW), lambda b, t: (b * nt + t, idx))


def _const_block(shape):
    return pl.BlockSpec(shape, lambda *_: tuple(0 for _ in shape))


def _packed_specs(tt, nt):
    qkw = GW // QK_PACK
    row = lambda b, t: b * nt + t
    return [pl.BlockSpec((tt, qkw), lambda b, t: (row(b, t), 0)),
            pl.BlockSpec((tt, qkw), lambda b, t: (row(b, t), 1)),
            pl.BlockSpec((tt, GW), lambda b, t: (row(b, t), (2 * qkw) // GW)),
            pl.BlockSpec((tt, GW), lambda b, t: (row(b, t), (2 * qkw) // GW + 1)),
            pl.BlockSpec((tt, LANES), lambda b, t: (row(b, t), (2 * qkw + 2 * GW) // LANES))]


def gla_mixer(proj, gate_w, gate_b, norm_g, *, batch, tt=512):
    t = proj.shape[0]
    s = t // batch
    tt = min(tt, s)
    nt = s // tt
    sums, lvl = _gla_tables()
    gb = functools.partial(_group_block, tt, nt)
    return pl.pallas_call(
        _gla_kernel,
        grid=(batch, nt),
        in_specs=_packed_specs(tt, nt)
        + [_const_block(gate_w.shape), _const_block(gate_b.shape),
           _const_block((1, LANES)), _const_block(sums.shape), _const_block(lvl.shape)],
        out_specs=gb(0),
        out_shape=jax.ShapeDtypeStruct((t, GW), BF16),
        scratch_shapes=[pltpu.VMEM((HEADS, LANES, LANES), F32)],
        compiler_params=_cparams(("parallel", "arbitrary")),
        name="gla",
    )(proj, proj, proj, proj, proj, gate_w, gate_b, norm_g.reshape(1, LANES),
      jnp.asarray(sums, BF16), jnp.asarray(lvl))


def hgrn_mixer(proj, lower_bound, norm_g, *, batch, tt=512):
    t = proj.shape[0]
    s = t // batch
    tt = min(tt, s)
    nt = s // tt
    sums, lvl = _gla_tables()
    gb = functools.partial(_group_block, tt, nt)
    return pl.pallas_call(
        _hgrn_kernel,
        grid=(batch, nt),
        in_specs=[gb(0), gb(1), gb(2), gb(3),
                  _const_block((1, GW)),
                  _const_block((1, LANES)), _const_block(sums.shape), _const_block(lvl.shape)],
        out_specs=gb(0),
        out_shape=jax.ShapeDtypeStruct((t, GW), BF16),
        scratch_shapes=[pltpu.VMEM((HEADS, LANES, LANES), F32)],
        compiler_params=_cparams(("parallel", "arbitrary")),
        name="hgrn",
    )(proj, proj, proj, proj, lower_bound.reshape(1, GW), norm_g.reshape(1, LANES),
      jnp.asarray(sums, BF16), jnp.asarray(lvl))


def _mlstm_kernel(q_ref, k_ref, v_ref, og_ref, if_ref, cw_q_ref, cw_k_ref, gb_ref, o_ref,
                  st_ref, m_ref, qx_ref, kx_ref):
    tt = q_ref.shape[0]

    @pl.when(pl.program_id(1) == 0)
    def _():
        st_ref[...] = jnp.zeros_like(st_ref)
        m_ref[...] = jnp.zeros_like(m_ref)
        qx_ref[0:CONV_PAD, :] = jnp.zeros((CONV_PAD, qx_ref.shape[1]), F32)
        kx_ref[0:CONV_PAD, :] = jnp.zeros((CONV_PAD, kx_ref.shape[1]), F32)

    qx_ref[CONV_PAD:CONV_PAD + tt, :] = q_ref[...]
    kx_ref[CONV_PAD:CONV_PAD + tt, :] = k_ref[...]

    c = CHUNK
    rows = PAIR * c

    def conv_silu(ext, cw, r0):
        xw = ext[pl.ds(r0, rows + CONV_PAD), :]
        acc = xw[CONV_PAD - (CONV_WIDTH - 1):CONV_PAD - (CONV_WIDTH - 1) + rows] * cw[0:1, :]
        for w in range(1, CONV_WIDTH):
            acc = acc + xw[CONV_PAD - (CONV_WIDTH - 1) + w:CONV_PAD - (CONV_WIDTH - 1) + w + rows] * cw[w:w + 1, :]
        return _silu(acc)

    row = lax.broadcasted_iota(jnp.int32, (rows, rows), 0)
    col = lax.broadcasted_iota(jnp.int32, (rows, rows), 1)
    causal = (col <= row) & (col >= (row // c) * c)
    eye = col == row
    tril3 = jnp.concatenate([jnp.where(causal, 1.0, 0.0).astype(BF16)] * 3, axis=1)
    chunk_of_row = lax.broadcasted_iota(jnp.int32, (rows, 1), 0) // c
    head_of_lane = lax.broadcasted_iota(jnp.int32, (rows, LANES), 1) // MLSTM_DK
    ones_col = jnp.where(lax.broadcasted_iota(jnp.int32, (rows, LANES), 1) == 0, 1.0, 0.0)
    gbias = gb_ref[...]

    def body(pi, carry):
        r0 = pl.multiple_of(pi * rows, rows)
        sl = pl.ds(r0, rows)
        gates = if_ref[sl, :] + gbias
        cum_all = _dot(tril3, _split3(_log_sigmoid(gates)))
        qc = _tiles(conv_silu(qx_ref, cw_q_ref, r0))
        kc = _tiles(conv_silu(kx_ref, cw_k_ref, r0) * MLSTM_DK ** -0.5)
        vs = _heads(v_ref[sl, :])
        og = _heads(_sigmoid(og_ref[sl, :]))
        outs = []
        for h in range(HEADS):
            i_col = gates[:, h:h + 1]
            cum = cum_all[:, HEADS + h:HEADS + h + 1]
            a_col = i_col - cum
            a_row = jnp.sum(jnp.where(eye, a_col, 0.0), axis=0, keepdims=True)
            m_in = [m_ref[h]]
            cum_last, lli = [], []
            for p in range(PAIR):
                cum_last.append(cum[(p + 1) * c - 1:(p + 1) * c, :])
                lli.append(cum_last[p] + a_col[p * c:(p + 1) * c])
                m_in.append(jnp.maximum(cum_last[p] + m_in[p], jnp.max(lli[p], axis=0, keepdims=True)))
            m_st = m_in[0]
            for p in range(1, PAIR):
                m_st = jnp.where(chunk_of_row >= p, m_in[p], m_st)
            log_intra = jnp.where(causal, cum + a_row, MASK_NEG)
            log_inter = cum + m_st
            m_t = jnp.maximum(log_inter, jnp.max(log_intra, axis=-1, keepdims=True))
            w_inter = jnp.exp(log_inter - m_t)
            w_intra = jnp.where(causal, jnp.exp(log_intra - m_t), 0.0)
            qb = jnp.where(head_of_lane == h % QK_PACK, qc[h // QK_PACK], 0.0).astype(BF16)
            kf = kc[h // QK_PACK]
            v_ext = jnp.concatenate([vs[h], ones_col], axis=-1).astype(BF16)
            scores = _dot_nt(qb, kf.astype(BF16)) * w_intra
            nd_intra = _dot(scores.astype(BF16), v_ext)
            st = st_ref[h]
            nd_inter = []
            for p in range(PAIR):
                rs = slice(p * c, (p + 1) * c)
                nd_inter.append(_dot(qb[rs], st.astype(BF16)))
                wk = jnp.exp(lli[p] - m_in[p + 1])
                st = (jnp.exp(cum_last[p] + m_in[p] - m_in[p + 1]) * st
                      + _dot_tn((kf[rs] * wk).astype(BF16), v_ext[rs]))
            st_ref[h] = st
            m_ref[h] = m_in[PAIR]
            nd = w_inter * jnp.concatenate(nd_inter, axis=0) + nd_intra
            den = jnp.maximum(jnp.abs(nd[:, LANES:LANES + 1]), jnp.exp(-m_t))
            outs.append(nd[:, :LANES] / den * og[h])
        _store_heads(o_ref, sl, outs)
        return carry

    lax.fori_loop(0, tt // rows, body, 0, unroll=CHUNK_UNROLL // PAIR)
    qx_ref[0:CONV_PAD, :] = qx_ref[tt:tt + CONV_PAD, :]
    kx_ref[0:CONV_PAD, :] = kx_ref[tt:tt + CONV_PAD, :]


def mlstm_mixer(proj, conv_w_q, conv_w_k, gate_b, *, batch, tt=512):
    t = proj.shape[0]
    s = t // batch
    tt = min(tt, s)
    nt = s // tt
    gb = functools.partial(_group_block, tt, nt)
    return pl.pallas_call(
        _mlstm_kernel,
        grid=(batch, nt),
        in_specs=_packed_specs(tt, nt)
        + [_const_block(conv_w_q.shape), _const_block(conv_w_k.shape), _const_block((1, LANES))],
        out_specs=gb(0),
        out_shape=jax.ShapeDtypeStruct((t, GW), BF16),
        scratch_shapes=[pltpu.VMEM((HEADS, LANES, 2 * LANES), F32), pltpu.VMEM((HEADS, 1, 1), F32),
                        pltpu.VMEM((CONV_PAD + tt, GW // QK_PACK), F32),
                        pltpu.VMEM((CONV_PAD + tt, GW // QK_PACK), F32)],
        compiler_params=_cparams(("parallel", "arbitrary")),
        name="mlstm",
    )(proj, proj, proj, proj, proj, conv_w_q, conv_w_k, gate_b)


LOG2E = math.log2(math.e)


def _diff_attn_kernel(sc_ref, q_ref, k_ref, vt_ref, bias_ref, ng_ref, o_ref,
                      s_ref, p_ref, mx_ref, m_ref, c_ref, a_ref):
    qi = pl.program_id(2)
    q = q_ref[...].astype(F32) * (DIFF_DK ** -0.5 * LOG2E)
    lane = lax.broadcasted_iota(jnp.int32, q.shape, 1)
    qs = (jnp.where(lane < DIFF_DK, q, 0.0).astype(BF16), jnp.where(lane >= DIFF_DK, q, 0.0).astype(BF16))
    m_ref[...] = jnp.full(m_ref.shape, MASK_NEG, F32)
    c_ref[...] = jnp.ones_like(c_ref)
    a_ref[...] = jnp.zeros_like(a_ref)
    p_ref[1] = jnp.zeros(p_ref.shape[1:], BF16)
    tk = vt_ref.shape[2]
    ones_rows = jnp.where(lax.broadcasted_iota(jnp.int32, (BF16_SUBLANES, tk), 0) == 0, 1.0, 0.0).astype(BF16)

    def scores(kt, slot):
        kk = k_ref[pl.ds(pl.multiple_of(kt * tk, tk), tk), :]
        tile = jnp.minimum(qi - kt, 2)
        for mp in range(2):
            s = _dot_nt(kk, qs[mp]) + bias_ref[tile]
            s_ref[slot, mp] = s
            mx_ref[slot, mp] = jnp.max(s, axis=0, keepdims=True)

    def values(kt, slot):
        vt = jnp.concatenate([vt_ref[kt], ones_rows], axis=0)
        for mp in range(2):
            a_ref[mp] = a_ref[mp] * c_ref[mp] + _dot(vt, p_ref[slot, mp])

    scores(0, 0)

    def step(ki, cur):
        nxt = 1 - cur
        scores(jnp.minimum(ki + 1, qi), nxt)
        values(jnp.maximum(ki - 1, 0), nxt)
        for mp in range(2):
            m_old = m_ref[mp]
            m_new = jnp.maximum(m_old, mx_ref[cur, mp])
            p_ref[cur, mp] = jnp.exp2(s_ref[cur, mp] - m_new).astype(BF16)
            c_ref[mp] = jnp.exp2(m_old - m_new)
            m_ref[mp] = m_new

    def pair(j, carry):
        step(2 * j, 0)
        step(2 * j + 1, 1)
        return carry

    n_tiles = qi + 1
    lax.fori_loop(0, n_tiles // 2, pair, 0)

    @pl.when(lax.rem(n_tiles, 2) == 1)
    def _():
        step(qi, 0)

    values(qi, lax.rem(qi, 2))
    lam = sc_ref[0]
    dv = o_ref.shape[1]
    out_t = (a_ref[0, 0:dv] / a_ref[0, dv:dv + 1] - lam * (a_ref[1, 0:dv] / a_ref[1, dv:dv + 1]))
    out_t = out_t * lax.rsqrt(jnp.mean(out_t * out_t, axis=0, keepdims=True) + LN_EPS)
    o_ref[...] = (out_t.T * ng_ref[...] * sc_ref[1]).astype(o_ref.dtype)


def diff_attn_mixer(qk, vt, bias_tiles, scalars, norm_g, *, batch, tq):
    t = qk.shape[0]
    s = t // batch
    nt = s // tq
    dve = LANES + BF16_SUBLANES
    return pl.pallas_call(
        _diff_attn_kernel,
        grid=(batch, HEADS, nt),
        in_specs=[pl.BlockSpec(memory_space=pltpu.SMEM),
                  pl.BlockSpec((tq, LANES), lambda b, h, t: (b * nt + t, h)),
                  pl.BlockSpec((s, LANES), lambda b, h, t: (b, HEADS + h)),
                  pl.BlockSpec((None, nt, LANES, tq), lambda b, h, t: (b * HEADS + h, 0, 0, 0)),
                  pl.BlockSpec((None, 3, tq, tq), lambda b, h, t: (h, 0, 0, 0)),
                  _const_block((1, LANES))],
        out_specs=pl.BlockSpec((tq, LANES), lambda b, h, t: (b * nt + t, h)),
        out_shape=jax.ShapeDtypeStruct((t, HEADS * LANES), BF16),
        scratch_shapes=[pltpu.VMEM((2, 2, tq, tq), F32), pltpu.VMEM((2, 2, tq, tq), BF16),
                        pltpu.VMEM((2, 2, 1, tq), F32), pltpu.VMEM((2, 1, tq), F32), pltpu.VMEM((2, 1, tq), F32),
                        pltpu.VMEM((2, dve, tq), F32)],
        compiler_params=_cparams(("parallel", "parallel", "arbitrary")),
        name="diff_attn",
    )(scalars, qk, qk, vt, bias_tiles, norm_g.reshape(1, LANES))


def _t5_bucket_map(rel):
    n = np.maximum(rel, 0)
    max_exact = T5_BUCKETS // 2
    large = max_exact + (np.log(np.maximum(n, 1).astype(np.float32) / np.float32(max_exact))
                         / np.float32(math.log(T5_MAX_DIST / max_exact)) * (T5_BUCKETS - max_exact)).astype(np.int32)
    large = np.clip(large, max_exact, T5_BUCKETS - 1)
    return np.where(n < max_exact, n, large).astype(np.int32)


def t5_bias_tiles(t5_table, tq):
    assert tq >= T5_MAX_DIST
    j = np.arange(tq)[:, None]
    i = np.arange(tq)[None, :]
    rel = np.stack([d * tq + i - j for d in range(3)])
    bucket = jnp.asarray(_t5_bucket_map(rel).astype(np.int8))
    table = t5_table.astype(F32)
    tiles = jnp.zeros((HEADS,) + rel.shape, F32)
    for b in range(T5_BUCKETS):
        tiles = jnp.where(bucket[None] == b, table[b][:, None, None, None], tiles)
    return jnp.where(jnp.asarray(rel >= 0)[None], tiles * LOG2E, MASK_NEG)


def _xattn_kernel(xb_ref, wq_ref, k_ref, v_ref, o_ref):
    d = xb_ref.shape[1]
    hd = d // N_XHEADS
    q = _dot(xb_ref[...], wq_ref[...]).astype(BF16)
    for h in range(N_XHEADS):
        hs = slice(h * hd, (h + 1) * hd)
        s = _dot_nt(q[:, hs], k_ref[:, hs]) * hd ** -0.5
        s = s - jnp.max(s, axis=-1, keepdims=True)
        p = jnp.exp(s)
        p = p / jnp.sum(p, axis=-1, keepdims=True)
        o_ref[:, hs] = _dot(p.astype(BF16), v_ref[:, hs]).astype(o_ref.dtype)


def xattn(xb, wq, kv, *, batch, tm=1024):
    t, d = xb.shape
    s = t // batch
    tm = min(tm, s)
    nt = s // tm
    n_mem = kv.shape[0] // batch
    return pl.pallas_call(
        _xattn_kernel,
        grid=(t // tm,),
        in_specs=[pl.BlockSpec((tm, d), lambda i: (i, 0)),
                  pl.BlockSpec((d, d), lambda i: (0, 0)),
                  pl.BlockSpec((n_mem, d), lambda i: (i // nt, 0)),
                  pl.BlockSpec((n_mem, d), lambda i: (i // nt, 1))],
        out_specs=pl.BlockSpec((tm, d), lambda i: (i, 0)),
        out_shape=jax.ShapeDtypeStruct((t, d), BF16),
        compiler_params=_cparams(("parallel",)),
        name="xattn",
    )(xb, wq, kv, kv)


def _pad_cols(w, n):
    return jnp.pad(w, [(0, 0)] * (w.ndim - 1) + [(0, n - w.shape[-1])])


def _split_in_proj(w_in):
    gw = HEADS * LANES
    sizes = (HEADS * GLA_DK, HEADS * GLA_DK, gw, GLA_GATE_RANK, gw,
             HEADS * HGRN_DK, HEADS * HGRN_DK, gw, gw,
             2 * HEADS * DIFF_DK, 2 * HEADS * DIFF_DK, gw,
             HEADS * MLSTM_DK, HEADS * MLSTM_DK, gw, 2 * HEADS, gw)
    assert sum(sizes) == w_in.shape[-1]
    (a_q, a_k, a_v, a_lr, a_r, b_q, b_f, b_i, b_g, c_q, c_k, c_v,
     d_q, d_k, d_v, d_if, d_o) = jnp.split(w_in, list(np.cumsum(sizes)[:-1]), axis=-1)
    w_gla = jnp.concatenate([a_q, a_k, a_v, a_r, _pad_cols(a_lr, LANES)], -1)
    w_hgrn = jnp.concatenate([b_q, b_f, b_i, b_g], -1)
    w_diff = jnp.concatenate([c_q, c_k, c_v], -1)
    w_mlstm = jnp.concatenate([d_q, d_k, d_v, d_o, _pad_cols(d_if, LANES)], -1)
    return [w.astype(BF16) for w in (w_gla, w_hgrn, w_diff, w_mlstm)]


def kernel(x, mem, ln_g, ln_b, ffn_w_in, ffn_w_out, w_in, w_out, gla_gate_w, gla_gate_b, gla_norm_g, hgrn_lb, hgrn_norm_g, diff_lambda, diff_norm_g, t5_table, mlstm_conv_w, mlstm_gate_b, xattn_w_q, xattn_w_kv, xattn_w_o):
    batch, seq, d = x.shape
    depth = ln_g.shape[0]
    alpha = (2 * depth) ** 0.25
    t = batch * seq

    sm = jax.nn.softmax(hgrn_lb.astype(F32), axis=0)
    lower_bounds = jnp.clip(jnp.cumsum(sm, axis=0) - sm[0], 0.0, 1.0 - 1e-6)
    tq = min(512, seq)
    bias_tiles = t5_bias_tiles(t5_table, tq)

    x = x.reshape(t, d)
    xb = x.astype(BF16)
    memb = mem.reshape(-1, d).astype(BF16)
    for l in range(depth):
        x, xb = ffn_ln(x, xb, ffn_w_in[l, 0].astype(BF16), ffn_w_out[l, 0].astype(BF16), ln_g[l, 0], ln_b[l, 0], alpha=alpha)

        w_gla, w_hgrn, w_diff, w_mlstm = _split_in_proj(w_in[l])
        o_a = gla_mixer(matmul(xb, w_gla), _pad_cols(gla_gate_w[l].T, LANES).T.astype(BF16),
                        gla_gate_b[l].reshape(1, -1), gla_norm_g[l], batch=batch)
        o_b = hgrn_mixer(matmul(xb, w_hgrn), lower_bounds[l], hgrn_norm_g[l], batch=batch)
        lambda_init = 0.8 - 0.6 * math.exp(-0.3 * l)
        lq1, lk1, lq2, lk2 = [diff_lambda[l, j].astype(F32) for j in range(4)]
        lam = jnp.exp(jnp.sum(lq1 * lk1)) - jnp.exp(jnp.sum(lq2 * lk2)) + lambda_init
        p_c = matmul(xb, w_diff, out_dtype=BF16)
        vt_c = (p_c[:, 2 * GW:].reshape(batch, seq // tq, tq, HEADS, LANES).transpose(0, 3, 1, 4, 2)
                .reshape(batch * HEADS, seq // tq, LANES, tq))
        o_c = diff_attn_mixer(p_c, vt_c, bias_tiles,
                              jnp.stack([lam, jnp.asarray(1.0 - lambda_init, F32)]).astype(F32), diff_norm_g[l], batch=batch, tq=tq)
        cw = mlstm_conv_w[l]
        o_d = mlstm_mixer(matmul(xb, w_mlstm), cw[:, :HEADS * MLSTM_DK], cw[:, HEADS * MLSTM_DK:],
                          _pad_cols(mlstm_gate_b[l].reshape(1, -1), LANES), batch=batch)
        x, xb = proj_ln(x, [o_a, o_b, o_c, o_d], w_out[l].astype(BF16), ln_g[l, 1], ln_b[l, 1], alpha=alpha)

        kv = matmul(memb, xattn_w_kv[l].astype(BF16), out_dtype=BF16, tn=1024)
        att = xattn(xb, xattn_w_q[l].astype(BF16), kv, batch=batch)
        x, xb = proj_ln(x, [att], xattn_w_o[l].astype(BF16), ln_g[l, 2], ln_b[l, 2], alpha=alpha)

        x, xb = ffn_ln(x, xb, ffn_w_in[l, 1].astype(BF16), ffn_w_out[l, 1].astype(BF16), ln_g[l, 3], ln_b[l, 3], alpha=alpha)
    return x.reshape(batch, seq, d)
```
